```python
import math
import jax, jax.numpy as jnp
from jax import lax
import numpy as np


D_MODEL = 1024
BATCH = 4
SEQ = 4096
DEPTH = 1

PLE_DIM = 256
D_MIX = D_MODEL
M_HEADS = 4
M_HEAD_DIM = D_MIX // 2 // M_HEADS
M_WIDTH = M_HEADS * M_HEAD_DIM
CHUNK = 64
CONV_WIDTH = 4
F_BIAS_LO = 3.0
F_BIAS_HI = 6.0
A_HEADS = 4
A_HEAD_DIM = (D_MIX - M_WIDTH) // A_HEADS
A_QK_DIM = A_HEAD_DIM // 2
A_WIDTH = A_HEADS * A_HEAD_DIM
ROPE_THETA = 500000.0
ROPE_DIM = A_QK_DIM // 4
Q_BLOCK = 128
D_FF = 4 * D_MODEL
EPS = 1e-6
SPLIT_SIZES = (M_WIDTH, M_WIDTH, M_WIDTH, M_WIDTH, M_HEADS, M_HEADS, A_WIDTH, A_WIDTH, A_WIDTH)
D_IN = 4 * M_WIDTH + 2 * M_HEADS + 3 * A_WIDTH

kernel_name = 'hybrid_mlstm_diffattn_block'


def rmsnorm(x, g):
    xf = x.astype(jnp.float32)
    y = xf * lax.rsqrt(jnp.mean(xf * xf, axis=-1, keepdims=True) + EPS)
    return y * g.astype(jnp.float32)


def split_columns(z):
    parts, off = [], 0
    for s in SPLIT_SIZES:
        parts.append(z[..., off:off + s])
        off += s
    return parts


def lambda_init_fn(layer):
    return 0.8 - 0.6 * math.exp(-0.3 * layer)


def rope_tables(seq):
    pos = jnp.arange(seq, dtype=jnp.float32)
    inv_freq = ROPE_THETA ** (-jnp.arange(0, ROPE_DIM, 2, dtype=jnp.float32) / ROPE_DIM)
    ang = pos[:, None] * inv_freq[None, :]
    return jnp.cos(ang), jnp.sin(ang)


def partial_rope(x, cos, sin):
    half = ROPE_DIM // 2
    c = cos[:, None, None, :]
    s = sin[:, None, None, :]
    x1 = x[..., :half]
    x2 = x[..., half:ROPE_DIM]
    return jnp.concatenate([x1 * c - x2 * s, x2 * c + x1 * s, x[..., ROPE_DIM:]], axis=-1)


def mlstm_mixer(mq, mk, mv, mo, mi, mf, conv_w, conv_b, igate_b, fgate_b, norm_g):
    B, S, _ = mq.shape
    nc = S // CHUNK
    qk = jnp.concatenate([mq, mk], axis=-1)
    C2 = qk.shape[-1]
    qk = lax.conv_general_dilated(qk, conv_w.astype(jnp.float32).reshape(CONV_WIDTH, 1, C2),
                                  window_strides=(1,), padding=[(CONV_WIDTH - 1, 0)],
                                  dimension_numbers=('NWC', 'WIO', 'NWC'),
                                  feature_group_count=C2) + conv_b
    qk = jax.nn.silu(qk)
    q = qk[..., :M_WIDTH] * (M_HEAD_DIM ** -0.5)
    k = qk[..., M_WIDTH:]

    def heads_to_chunks(t):
        return t.reshape(B, nc, CHUNK, M_HEADS, M_HEAD_DIM).transpose(1, 0, 3, 2, 4)

    def gates_to_chunks(t):
        return t.reshape(B, nc, CHUNK, M_HEADS).transpose(1, 0, 3, 2)

    qc, kc, vc = heads_to_chunks(q), heads_to_chunks(k), heads_to_chunks(mv)
    li = gates_to_chunks(mi + igate_b)
    lf = jax.nn.log_sigmoid(gates_to_chunks(mf + fgate_b))
    bcum = jnp.cumsum(lf, axis=-1)
    causal = jnp.tril(jnp.ones((CHUNK, CHUNK), dtype=bool))

    def step(carry, xs):
        Cm, n, m = carry
        qt, kt, vt, lit, bt = xs
        Dlog = bt[..., :, None] - bt[..., None, :] + lit[..., None, :]
        Dlog = jnp.where(causal, Dlog, -jnp.inf)
        inter = bt + m[..., None]
        m_t = jnp.maximum(inter, jnp.max(Dlog, axis=-1))
        w_inter = jnp.exp(inter - m_t)
        Smat = jnp.einsum('bhtd,bhsd->bhts', qt, kt) * jnp.exp(Dlog - m_t[..., None])
        num = w_inter[..., None] * jnp.einsum('bhvd,bhtd->bhtv', Cm, qt) \
            + jnp.einsum('bhts,bhsv->bhtv', Smat, vt)
        den = w_inter * jnp.einsum('bhd,bhtd->bht', n, qt) + jnp.sum(Smat, axis=-1)
        h = num / jnp.maximum(jnp.abs(den), jnp.exp(-m_t))[..., None]
        bL = bt[..., -1]
        g = bL[..., None] - bt + lit
        m_new = jnp.maximum(bL + m, jnp.max(g, axis=-1))
        a_s = jnp.exp(g - m_new[..., None])
        decay = jnp.exp(bL + m - m_new)
        C_new = decay[..., None, None] * Cm + jnp.einsum('bhs,bhsv,bhsd->bhvd', a_s, vt, kt)
        n_new = decay[..., None] * n + jnp.einsum('bhs,bhsd->bhd', a_s, kt)
        return (C_new, n_new, m_new), h

    init = (jnp.zeros((B, M_HEADS, M_HEAD_DIM, M_HEAD_DIM), jnp.float32),
            jnp.zeros((B, M_HEADS, M_HEAD_DIM), jnp.float32),
            jnp.zeros((B, M_HEADS), jnp.float32))
    _, hs = lax.scan(step, init, (qc, kc, vc, li, bcum))
    hs = hs.transpose(1, 0, 3, 2, 4).reshape(B, S, M_HEADS, M_HEAD_DIM)
    hs = rmsnorm(hs, norm_g.reshape(M_HEADS, M_HEAD_DIM)).reshape(B, S, M_WIDTH)
    return jax.nn.sigmoid(mo) * hs


def diff_attn_mixer(aq, ak, av, q_norm_g, k_norm_g, lq1, lk1, lq2, lk2, sub_g, lam_init, cos, sin):
    B, S, _ = aq.shape
    nb = S // Q_BLOCK
    q = rmsnorm(aq.reshape(B, S, A_HEADS, 2, A_QK_DIM), q_norm_g)
    k = rmsnorm(ak.reshape(B, S, A_HEADS, 2, A_QK_DIM), k_norm_g)
    q = partial_rope(q, cos, sin).transpose(0, 2, 3, 1, 4)
    k = partial_rope(k, cos, sin).transpose(0, 2, 3, 1, 4)
    v = av.reshape(B, S, A_HEADS, A_HEAD_DIM).transpose(0, 2, 1, 3)
    lam = jnp.exp(jnp.sum(lq1 * lk1)) - jnp.exp(jnp.sum(lq2 * lk2)) + lam_init
    scale = A_QK_DIM ** -0.5
    qb = q.reshape(B, A_HEADS, 2, nb, Q_BLOCK, A_QK_DIM).transpose(3, 0, 1, 2, 4, 5)
    kpos = jnp.arange(S)

    def block(args):
        qblk, bi = args
        s = jnp.einsum('bhmqd,bhmkd->bhmqk', qblk, k) * scale
        qpos = bi * Q_BLOCK + jnp.arange(Q_BLOCK)
        mask = kpos[None, :] <= qpos[:, None]
        a = jax.nn.softmax(jnp.where(mask, s, -jnp.inf), axis=-1)
        a = a[:, :, 0] - lam * a[:, :, 1]
        return jnp.einsum('bhqk,bhkd->bhqd', a, v)

    o = lax.map(block, (qb, jnp.arange(nb)))
    o = o.transpose(1, 0, 3, 2, 4).reshape(B, S, A_HEADS, A_HEAD_DIM)
    o = rmsnorm(o, sub_g.reshape(A_HEADS, A_HEAD_DIM)) * (1.0 - lam_init)
    return o.reshape(B, S, A_WIDTH)


def setup_inputs(seed: int = 0) -> dict:
    key = jax.random.key(seed)
    ks = jax.random.split(key, 24)
    f32 = jnp.float32

    def nrm(k, shape, scale):
        return jax.random.normal(k, shape, f32) * scale

    def gain(k, shape):
        return 1.0 + 0.02 * jax.random.normal(k, shape, f32)

    fgate_b = jnp.linspace(F_BIAS_LO, F_BIAS_HI, M_HEADS, dtype=f32)[None, :] \
        + 0.1 * jax.random.normal(ks[5], (DEPTH, M_HEADS), f32)
    return {
        'x': nrm(ks[0], (BATCH, SEQ, D_MODEL), 1.0),
        'p': nrm(ks[1], (DEPTH, BATCH, SEQ, PLE_DIM), 1.0),
        'attn_norm_g': gain(ks[2], (DEPTH, D_MODEL)),
        'w_in': nrm(ks[3], (DEPTH, D_MODEL, D_IN), D_MODEL ** -0.5),
        'conv_w': nrm(ks[4], (DEPTH, CONV_WIDTH, 2 * M_WIDTH), CONV_WIDTH ** -0.5),
        'conv_b': nrm(ks[6], (DEPTH, 2 * M_WIDTH), 0.02),
        'igate_b': nrm(ks[7], (DEPTH, M_HEADS), 0.1),
        'fgate_b': fgate_b,
        'mlstm_norm_g': gain(ks[8], (DEPTH, M_WIDTH)),
        'q_norm_g': gain(ks[9], (DEPTH, A_QK_DIM)),
        'k_norm_g': gain(ks[10], (DEPTH, A_QK_DIM)),
        'lambda_q1': nrm(ks[11], (DEPTH, A_QK_DIM), 0.1),
        'lambda_k1': nrm(ks[12], (DEPTH, A_QK_DIM), 0.1),
        'lambda_q2': nrm(ks[13], (DEPTH, A_QK_DIM), 0.1),
        'lambda_k2': nrm(ks[14], (DEPTH, A_QK_DIM), 0.1),
        'attn_sub_norm_g': gain(ks[15], (DEPTH, A_WIDTH)),
        'w_out': nrm(ks[16], (DEPTH, D_MIX, D_MODEL), D_MIX ** -0.5),
        'mlp_norm_g': gain(ks[17], (DEPTH, D_MODEL)),
        'w_up': nrm(ks[18], (DEPTH, D_MODEL, D_FF), D_MODEL ** -0.5),
        'w_down': nrm(ks[19], (DEPTH, D_FF, D_MODEL), D_FF ** -0.5),
        'ple_norm_g': gain(ks[20], (DEPTH, D_MODEL)),
        'w_ple_gate': nrm(ks[21], (DEPTH, D_MODEL, D_MODEL), D_MODEL ** -0.5),
        'w_ple_proj': nrm(ks[22], (DEPTH, PLE_DIM, D_MODEL), PLE_DIM ** -0.5),
    }


def reference(x, p, attn_norm_g, w_in, conv_w, conv_b, igate_b, fgate_b, mlstm_norm_g,
              q_norm_g, k_norm_g, lambda_q1, lambda_k1, lambda_q2, lambda_k2,
              attn_sub_norm_g, w_out, mlp_norm_g, w_up, w_down, ple_norm_g,
              w_ple_gate, w_ple_proj):
    out_dtype = x.dtype
    B, S, _ = x.shape
    h = x.astype(jnp.float32)
    cos, sin = rope_tables(S)
    for l in range(DEPTH):
        u = rmsnorm(h, attn_norm_g[l])
        z = jnp.matmul(u, w_in[l].astype(jnp.float32))
        mq, mk, mv, mo, mi, mf, aq, ak, av = split_columns(z)
        y_m = mlstm_mixer(mq, mk, mv, mo, mi, mf, conv_w[l], conv_b[l],
                          igate_b[l], fgate_b[l], mlstm_norm_g[l])
        y_a = diff_attn_mixer(aq, ak, av, q_norm_g[l], k_norm_g[l],
                              lambda_q1[l].astype(jnp.float32), lambda_k1[l].astype(jnp.float32),
                              lambda_q2[l].astype(jnp.float32), lambda_k2[l].astype(jnp.float32),
                              attn_sub_norm_g[l], lambda_init_fn(l), cos, sin)
        y = jnp.concatenate([y_m, y_a], axis=-1)
        h = h + jnp.matmul(y, w_out[l].astype(jnp.float32))
        u2 = rmsnorm(h, mlp_norm_g[l])
        hid = jnp.square(jax.nn.relu(jnp.matmul(u2, w_up[l].astype(jnp.float32))))
        h = h + jnp.matmul(hid, w_down[l].astype(jnp.float32))
        gate = jax.nn.sigmoid(jnp.matmul(rmsnorm(h, ple_norm_g[l]), w_ple_gate[l].astype(jnp.float32)))
        e = jnp.matmul(p[l].astype(jnp.float32), w_ple_proj[l].astype(jnp.float32))
        h = h + gate * e
    return h.astype(out_dtype)
```

```python
import functools
import math

import jax
import jax.numpy as jnp
from jax import lax
from jax.experimental import pallas as pl
from jax.experimental.pallas import tpu as pltpu

F32 = jnp.float32
BF16 = jnp.bfloat16

EPS = 1e-6
LANES = 128
M_HEADS = 4
M_HEAD_DIM = 128
M_WIDTH = M_HEADS * M_HEAD_DIM
CONV_WIDTH = 4
A_HEADS = 4
A_HEAD_DIM = 128
A_QK_DIM = 64
A_WIDTH = A_HEADS * A_HEAD_DIM
ROPE_THETA = 500000.0
ROPE_DIM = A_QK_DIM // 4
ROPE_HALF = ROPE_DIM // 2
LAMBDA_INIT = 0.8 - 0.6 * math.exp(-0.3 * 0)

COL_MQ, COL_MK, COL_MV, COL_MO, COL_AQ, COL_AK, COL_AV = range(7)
N_GROUPS = 7
Z_WIDTH = N_GROUPS * M_WIDTH

PROJ_ROWS = 512
MLSTM_CHUNK = 256
ATTN_BLOCK = 256
FFN_ROWS = 512
FF_CHUNK = 1024
CARRY_ROWS = 8
NEG_BIG = -1e30
VMEM_LIMIT = 56 * 1024 * 1024


def _const_spec(shape):
    return pl.BlockSpec(shape, lambda *_: (0,) * len(shape), pipeline_mode=pl.Buffered(1))


def _group_mean_square(sq, lo_mask):
    tot = jnp.sum(sq, axis=-1, keepdims=True)
    lo = jnp.sum(jnp.where(lo_mask, sq, 0.0), axis=-1, keepdims=True)
    return jnp.where(lo_mask, lo, tot - lo) * (1.0 / A_QK_DIM)


def _qk_norm_rope(z, gain, ct, su, sd, scale):
    lane = lax.broadcasted_iota(jnp.int32, (1, LANES), 1)
    lo_mask = lane < A_QK_DIM
    outs = []
    for j in range(z.shape[1] // LANES):
        blk = z[:, j * LANES:(j + 1) * LANES]
        ms = _group_mean_square(blk * blk, lo_mask)
        xg = blk * lax.rsqrt(ms + EPS) * gain
        up = pltpu.roll(xg, ROPE_HALF, axis=1)
        dn = pltpu.roll(xg, LANES - ROPE_HALF, axis=1)
        outs.append((xg * ct + up * su + dn * sd) * scale)
    return jnp.concatenate(outs, axis=1)


def _log_sigmoid(x):
    return jnp.minimum(x, 0.0) - jnp.log1p(jnp.exp(-jnp.abs(x)))


def _proj_kernel(x_ref, g_ref, w_ref, wg_ref, gb_ref, qg_ref, kg_ref, ct_ref, su_ref, sd_ref,
                 z_ref, gc_ref, gr_ref):
    x = x_ref[...]
    ms = jnp.mean(x * x, axis=-1, keepdims=True)
    u = (x * lax.rsqrt(ms + EPS) * g_ref[...]).astype(BF16)
    ct, su, sd = ct_ref[...], su_ref[...], sd_ref[...]
    for c in range(N_GROUPS):
        cols = slice(c * M_WIDTH, (c + 1) * M_WIDTH)
        zc = jnp.dot(u, w_ref[:, cols], preferred_element_type=F32)
        if c == COL_AQ:
            zc = _qk_norm_rope(zc, qg_ref[...], ct, su, sd, A_QK_DIM ** -0.5)
        elif c == COL_AK:
            zc = _qk_norm_rope(zc, kg_ref[...], ct, su, sd, 1.0)
        z_ref[:, cols] = zc.astype(BF16)
    gz = jnp.dot(u, wg_ref[...], preferred_element_type=F32) + gb_ref[...]
    lane = lax.broadcasted_iota(jnp.int32, (1, LANES), 1)
    gz = jnp.where(lane >= M_HEADS, _log_sigmoid(gz), gz)
    gc_ref[...] = gz
    gr_ref[...] = gz.T[:2 * M_HEADS, :]


def _project(x2, attn_norm_g, w_main, w_gate, gate_b, q_gain, k_gain, ct, su, sd, seq):
    tokens = x2.shape[0]
    d_model = x2.shape[1]
    tm = PROJ_ROWS
    pos_blocks = seq // tm
    row_spec = lambda w: pl.BlockSpec((tm, w), lambda i: (i, 0))
    tab_spec = pl.BlockSpec((tm, LANES), lambda i: (i % pos_blocks, 0))
    return pl.pallas_call(
        _proj_kernel,
        grid=(tokens // tm,),
        in_specs=[row_spec(d_model), _const_spec((1, d_model)), _const_spec(w_main.shape),
                  _const_spec(w_gate.shape), _const_spec((1, LANES)), _const_spec((1, LANES)),
                  _const_spec((1, LANES)), tab_spec, tab_spec, tab_spec],
        out_specs=[row_spec(Z_WIDTH), row_spec(LANES),
                   pl.BlockSpec((2 * M_HEADS, tm), lambda i: (0, i))],
        out_shape=[jax.ShapeDtypeStruct((tokens, Z_WIDTH), BF16),
                   jax.ShapeDtypeStruct((tokens, LANES), F32),
                   jax.ShapeDtypeStruct((2 * M_HEADS, tokens), F32)],
        compiler_params=pltpu.CompilerParams(dimension_semantics=("parallel",),
                                             vmem_limit_bytes=VMEM_LIMIT),
        name="proj",
    )(x2, attn_norm_g, w_main, w_gate, gate_b, q_gain, k_gain, ct, su, sd)


def _mlstm_kernel(qk_ref, v_ref, o_ref, gc_ref, gr_ref, cw_ref, cb_ref, ng_ref, y_ref,
                  xbuf_ref, state_ref, m_ref):
    L = MLSTM_CHUNK
    D = M_HEAD_DIM

    @pl.when(pl.program_id(1) == 0)
    def _():
        xbuf_ref[0:CARRY_ROWS, :] = jnp.zeros((CARRY_ROWS, 2 * M_WIDTH), F32)
        state_ref[...] = jnp.zeros_like(state_ref)
        m_ref[...] = jnp.zeros_like(m_ref)

    xbuf_ref[CARRY_ROWS:, :] = qk_ref[...].astype(F32)
    cw = cw_ref[...]
    conv = cb_ref[...] + cw[CONV_WIDTH - 1:CONV_WIDTH, :] * xbuf_ref[CARRY_ROWS:, :]
    for back in range(1, CONV_WIDTH):
        tap = CONV_WIDTH - 1 - back
        conv = conv + cw[tap:tap + 1, :] * xbuf_ref[pl.ds(CARRY_ROWS - back, L), :]
    xbuf_ref[0:CARRY_ROWS, :] = xbuf_ref[L:L + CARRY_ROWS, :]
    qk = conv * jax.nn.sigmoid(conv)

    gc = gc_ref[...]
    gr = gr_ref[...]
    row = lax.broadcasted_iota(jnp.int32, (L, L), 0)
    col = lax.broadcasted_iota(jnp.int32, (L, L), 1)
    causal = col <= row
    lower = causal.astype(F32)
    upper = (row <= col).astype(F32)
    bc_all = jnp.dot(lower, gc, preferred_element_type=F32, precision=lax.Precision.HIGHEST)
    br_all = jnp.dot(gr, upper, preferred_element_type=F32, precision=lax.Precision.HIGHEST)
    lane_aug = lax.broadcasted_iota(jnp.int32, (L, D), 1)
    ones_col = jnp.where(lane_aug == 0, 1.0, 0.0).astype(BF16)

    for h in range(M_HEADS):
        hs = slice(h * D, (h + 1) * D)
        q = (qk[:, hs] * (D ** -0.5)).astype(BF16)
        k = qk[:, M_WIDTH + h * D:M_WIDTH + (h + 1) * D]
        v_aug = jnp.concatenate([v_ref[:, hs], ones_col], axis=1)
        li_c = gc[:, h:h + 1]
        b_c = bc_all[:, M_HEADS + h:M_HEADS + h + 1]
        li_r = gr[h:h + 1, :]
        b_r = br_all[M_HEADS + h:M_HEADS + h + 1, :]
        m_prev = m_ref[h:h + 1, 0:1]
        c_aug = state_ref[h]

        dlog = jnp.where(causal, b_c - b_r + li_r, NEG_BIG)
        inter = b_c + m_prev
        m_t = jnp.maximum(inter, jnp.max(dlog, axis=-1, keepdims=True))
        w_inter = jnp.exp(inter - m_t)
        s = lax.dot_general(q, k.astype(BF16), (((1,), (1,)), ((), ())),
                            preferred_element_type=F32)
        smat = (s * jnp.exp(dlog - m_t)).astype(BF16)
        tot = w_inter * jnp.dot(q, c_aug.astype(BF16), preferred_element_type=F32) \
            + jnp.dot(smat, v_aug, preferred_element_type=F32)
        den = tot[:, D:D + 1]
        hh = tot[:, :D] / jnp.maximum(jnp.abs(den), jnp.exp(-m_t))

        b_last = b_c[L - 1:L, :]
        g_c = b_last - b_c + li_c
        m_new = jnp.maximum(b_last + m_prev, jnp.max(g_c, axis=0, keepdims=True))
        a_c = jnp.exp(g_c - m_new)
        decay = jnp.exp(b_last + m_prev - m_new)
        k_scaled = (k * a_c).astype(BF16)
        state_ref[h] = decay * c_aug + lax.dot_general(
            k_scaled, v_aug, (((0,), (0,)), ((), ())), preferred_element_type=F32)
        m_ref[h:h + 1, :] = jnp.broadcast_to(m_new, (1, LANES))

        hn = hh * lax.rsqrt(jnp.mean(hh * hh, axis=-1, keepdims=True) + EPS) * ng_ref[:, hs]
        y_ref[:, hs] = (jax.nn.sigmoid(o_ref[:, hs].astype(F32)) * hn).astype(BF16)


def _mlstm(z, gates_c, gates_r, conv_w, conv_b, norm_g, batch, seq):
    L = MLSTM_CHUNK
    nt = seq // L
    tokens = batch * seq
    blk = M_WIDTH
    return pl.pallas_call(
        _mlstm_kernel,
        grid=(batch, nt),
        in_specs=[pl.BlockSpec((L, 2 * blk), lambda b, t: (b * nt + t, 0)),
                  pl.BlockSpec((L, blk), lambda b, t: (b * nt + t, COL_MV)),
                  pl.BlockSpec((L, blk), lambda b, t: (b * nt + t, COL_MO)),
                  pl.BlockSpec((L, LANES), lambda b, t: (b * nt + t, 0)),
                  pl.BlockSpec((2 * M_HEADS, L), lambda b, t: (0, b * nt + t)),
                  _const_spec((CONV_WIDTH, 2 * blk)), _const_spec((1, 2 * blk)),
                  _const_spec((1, blk))],
        out_specs=pl.BlockSpec((L, blk), lambda b, t: (b * nt + t, 0)),
        out_shape=jax.ShapeDtypeStruct((tokens, blk), BF16),
        scratch_shapes=[pltpu.VMEM((L + CARRY_ROWS, 2 * blk), F32),
                        pltpu.VMEM((M_HEADS, M_HEAD_DIM, 2 * M_HEAD_DIM), F32),
                        pltpu.VMEM((2 * M_HEADS, LANES), F32)],
        compiler_params=pltpu.CompilerParams(dimension_semantics=("parallel", "arbitrary"),
                                             vmem_limit_bytes=VMEM_LIMIT),
        name="mlstm",
    )(z, z, z, gates_c, gates_r, conv_w, conv_b, norm_g)


def _attn_kernel(q_ref, k_ref, v_ref, lam_ref, sg_ref, y_ref, m_ref, l_ref, acc_ref):
    T = ATTN_BLOCK
    i = pl.program_id(2)
    lane = lax.broadcasted_iota(jnp.int32, (1, LANES), 1)
    q = q_ref[...]
    zero = jnp.zeros_like(q)
    qs = jnp.concatenate([jnp.where(lane < A_QK_DIM, q, zero),
                          jnp.where(lane >= A_QK_DIM, q, zero)], axis=0)
    m_ref[...] = jnp.full_like(m_ref, NEG_BIG)
    l_ref[...] = jnp.zeros_like(l_ref)
    acc_ref[...] = jnp.zeros_like(acc_ref)

    def step(j, masked):
        start = pl.multiple_of(j * T, T)
        k = k_ref[pl.ds(start, T), :]
        v = v_ref[pl.ds(start, T), :]
        s = lax.dot_general(qs, k, (((1,), (1,)), ((), ())), preferred_element_type=F32)
        if masked:
            r = lax.broadcasted_iota(jnp.int32, (2 * T, T), 0)
            c = lax.broadcasted_iota(jnp.int32, (2 * T, T), 1)
            s = jnp.where(c <= jnp.where(r >= T, r - T, r), s, NEG_BIG)
        m_old = m_ref[...]
        m_new = jnp.maximum(m_old, jnp.max(s, axis=-1, keepdims=True))
        alpha = jnp.exp(m_old - m_new)
        p = jnp.exp(s - m_new)
        l_ref[...] = alpha * l_ref[...] + jnp.sum(p, axis=-1, keepdims=True)
        acc_ref[...] = alpha * acc_ref[...] + jnp.dot(p.astype(BF16), v,
                                                      preferred_element_type=F32)
        m_ref[...] = m_new

    def body(j, carry):
        step(j, False)
        return carry

    lax.fori_loop(0, i, body, 0)
    step(i, True)

    lq1, lk1, lq2, lk2 = lam_ref[0:1, :], lam_ref[1:2, :], lam_ref[2:3, :], lam_ref[3:4, :]
    lam = jnp.exp(jnp.sum(lq1 * lk1, axis=-1, keepdims=True)) \
        - jnp.exp(jnp.sum(lq2 * lk2, axis=-1, keepdims=True)) + LAMBDA_INIT
    o = acc_ref[...] / l_ref[...]
    o = o[:T, :] - lam * o[T:, :]
    on = o * lax.rsqrt(jnp.mean(o * o, axis=-1, keepdims=True) + EPS) * sg_ref[...]
    y_ref[...] = (on * (1.0 - LAMBDA_INIT)).astype(BF16)


def _attention(z, lam_vecs, sub_g, batch, seq):
    T = ATTN_BLOCK
    nq = seq // T
    tokens = batch * seq
    hb = M_WIDTH // LANES
    return pl.pallas_call(
        _attn_kernel,
        grid=(batch, A_HEADS, nq),
        in_specs=[pl.BlockSpec((T, LANES), lambda b, h, i: (b * nq + i, COL_AQ * hb + h)),
                  pl.BlockSpec((seq, LANES), lambda b, h, i: (b, COL_AK * hb + h)),
                  pl.BlockSpec((seq, LANES), lambda b, h, i: (b, COL_AV * hb + h)),
                  _const_spec((4, LANES)),
                  pl.BlockSpec((1, LANES), lambda b, h, i: (0, h))],
        out_specs=pl.BlockSpec((T, LANES), lambda b, h, i: (b * nq + i, h)),
        out_shape=jax.ShapeDtypeStruct((tokens, A_WIDTH), BF16),
        scratch_shapes=[pltpu.VMEM((2 * T, 1), F32), pltpu.VMEM((2 * T, 1), F32),
                        pltpu.VMEM((2 * T, LANES), F32)],
        compiler_params=pltpu.CompilerParams(
            dimension_semantics=("parallel", "parallel", "arbitrary"),
            vmem_limit_bytes=VMEM_LIMIT),
        name="diff_attn",
    )(z, z, z, lam_vecs, sub_g)


def _rms(x, g):
    return x * lax.rsqrt(jnp.mean(x * x, axis=-1, keepdims=True) + EPS) * g


def _ffn_kernel(x_ref, ym_ref, ya_ref, p_ref, wo_ref, g2_ref, wu_ref, wd_ref, g3_ref, wg_ref,
                wp_ref, out_ref):
    h = x_ref[...] \
        + jnp.dot(ym_ref[...], wo_ref[0:M_WIDTH, :], preferred_element_type=F32) \
        + jnp.dot(ya_ref[...], wo_ref[M_WIDTH:, :], preferred_element_type=F32)
    u2 = _rms(h, g2_ref[...]).astype(BF16)
    mlp = None
    for c in range(wu_ref.shape[1] // FF_CHUNK):
        cols = slice(c * FF_CHUNK, (c + 1) * FF_CHUNK)
        hid = jnp.maximum(jnp.dot(u2, wu_ref[:, cols], preferred_element_type=F32), 0.0)
        part = jnp.dot((hid * hid).astype(BF16), wd_ref[cols, :], preferred_element_type=F32)
        mlp = part if mlp is None else mlp + part
    h = h + mlp
    u3 = _rms(h, g3_ref[...]).astype(BF16)
    gate = jax.nn.sigmoid(jnp.dot(u3, wg_ref[...], preferred_element_type=F32))
    e = jnp.dot(p_ref[...].astype(BF16), wp_ref[...], preferred_element_type=F32)
    out_ref[...] = h + gate * e


def _ffn(x2, y_m, y_a, p2, w_out, g2, w_up, w_down, g3, w_gate, w_ple):
    tokens, d_model = x2.shape
    tm = FFN_ROWS
    row_spec = lambda w: pl.BlockSpec((tm, w), lambda i: (i, 0))
    return pl.pallas_call(
        _ffn_kernel,
        grid=(tokens // tm,),
        in_specs=[row_spec(d_model), row_spec(M_WIDTH), row_spec(A_WIDTH), row_spec(p2.shape[1]),
                  _const_spec(w_out.shape), _const_spec((1, d_model)), _const_spec(w_up.shape),
                  _const_spec(w_down.shape), _const_spec((1, d_model)),
                  _const_spec(w_gate.shape), _const_spec(w_ple.shape)],
        out_specs=row_spec(d_model),
        out_shape=jax.ShapeDtypeStruct((tokens, d_model), F32),
        compiler_params=pltpu.CompilerParams(dimension_semantics=("parallel",),
                                             vmem_limit_bytes=VMEM_LIMIT),
        name="ffn",
    )(x2, y_m, y_a, p2, w_out, g2, w_up, w_down, g3, w_gate, w_ple)


def _rope_tables(seq):
    pos = jnp.arange(seq, dtype=F32)
    inv_freq = ROPE_THETA ** (-jnp.arange(0, ROPE_DIM, 2, dtype=F32) / ROPE_DIM)
    ang = pos[:, None] * inv_freq[None, :]
    cos, sin = jnp.cos(ang), jnp.sin(ang)
    pad = jnp.zeros((seq, A_QK_DIM - ROPE_DIM), F32)
    zero8 = jnp.zeros((seq, ROPE_HALF), F32)
    ct = jnp.concatenate([cos, cos, pad + 1.0], axis=1)
    su = jnp.concatenate([zero8, sin, pad], axis=1)
    sd = jnp.concatenate([-sin, zero8, pad], axis=1)
    rep = LANES // A_QK_DIM
    return tuple(jnp.tile(t, (1, rep)) for t in (ct, su, sd))


def kernel(x, p, attn_norm_g, w_in, conv_w, conv_b, igate_b, fgate_b, mlstm_norm_g, q_norm_g,
           k_norm_g, lambda_q1, lambda_k1, lambda_q2, lambda_k2, attn_sub_norm_g, w_out,
           mlp_norm_g, w_up, w_down, ple_norm_g, w_ple_gate, w_ple_proj):
    batch, seq, d_model = x.shape
    assert w_in.shape[0] == 1, "single-layer block"
    assert seq % PROJ_ROWS == 0 and seq % MLSTM_CHUNK == 0 and seq % ATTN_BLOCK == 0
    tokens = batch * seq
    x2 = x.reshape(tokens, d_model).astype(F32)
    p2 = p[0].reshape(tokens, p.shape[-1])

    w = w_in[0]
    gate_lo = 4 * M_WIDTH
    gate_hi = gate_lo + 2 * M_HEADS
    w_main = jnp.concatenate([w[:, :gate_lo], w[:, gate_hi:]], axis=1).astype(BF16)
    w_gatecols = jnp.pad(w[:, gate_lo:gate_hi], ((0, 0), (0, LANES - 2 * M_HEADS))).astype(BF16)
    gate_b = jnp.pad(jnp.concatenate([igate_b[0], fgate_b[0]]),
                     (0, LANES - 2 * M_HEADS)).reshape(1, LANES).astype(F32)
    rep = LANES // A_QK_DIM
    q_gain = jnp.tile(q_norm_g[0].astype(F32), rep).reshape(1, LANES)
    k_gain = jnp.tile(k_norm_g[0].astype(F32), rep).reshape(1, LANES)
    ct, su, sd = _rope_tables(seq)

    z, gates_c, gates_r = _project(x2, attn_norm_g[0].reshape(1, d_model).astype(F32), w_main,
                                   w_gatecols, gate_b, q_gain, k_gain, ct, su, sd, seq)

    y_m = _mlstm(z, gates_c, gates_r, conv_w[0].astype(F32),
                 conv_b[0].reshape(1, -1).astype(F32),
                 mlstm_norm_g[0].reshape(1, -1).astype(F32), batch, seq)

    lam_vecs = jnp.pad(jnp.stack([lambda_q1[0], lambda_k1[0], lambda_q2[0], lambda_k2[0]]),
                       ((0, 0), (0, LANES - A_QK_DIM))).astype(F32)
    y_a = _attention(z, lam_vecs, attn_sub_norm_g[0].reshape(1, -1).astype(F32), batch, seq)

    out = _ffn(x2, y_m, y_a, p2, w_out[0].astype(BF16),
               mlp_norm_g[0].reshape(1, d_model).astype(F32), w_up[0].astype(BF16),
               w_down[0].astype(BF16), ple_norm_g[0].reshape(1, d_model).astype(F32),
               w_ple_gate[0].astype(BF16), w_ple_proj[0].astype(BF16))
    return out.reshape(batch, seq, d_model).astype(x.dtype)
```

```python
import functools
import math

import jax
import jax.numpy as jnp
from jax import lax
from jax.experimental import pallas as pl
from jax.experimental.pallas import tpu as pltpu

F32 = jnp.float32
BF16 = jnp.bfloat16

EPS = 1e-6
LANES = 128
M_HEADS = 4
M_HEAD_DIM = 128
M_WIDTH = M_HEADS * M_HEAD_DIM
CONV_WIDTH = 4
A_HEADS = 4
A_HEAD_DIM = 128
A_QK_DIM = 64
A_WIDTH = A_HEADS * A_HEAD_DIM
ROPE_THETA = 500000.0
ROPE_DIM = A_QK_DIM // 4
ROPE_HALF = ROPE_DIM // 2
LAMBDA_INIT = 0.8 - 0.6 * math.exp(-0.3 * 0)

COL_MQ, COL_MK, COL_MV, COL_MO, COL_AQ, COL_AK, COL_AV = range(7)
N_GROUPS = 7
Z_WIDTH = N_GROUPS * M_WIDTH

PROJ_ROWS = 512
MLSTM_CHUNK = 256
ATTN_K_BLOCK = 256
ATTN_Q_BLOCK = 2 * ATTN_K_BLOCK
FFN_ROWS = 512
FF_CHUNK = 1024
CARRY_ROWS = 8
NEG_BIG = -1e30
VMEM_LIMIT = 56 * 1024 * 1024


def _const_spec(shape):
    return pl.BlockSpec(shape, lambda *_: (0,) * len(shape), pipeline_mode=pl.Buffered(1))


def _group_mean_square(sq, lo_mask):
    tot = jnp.sum(sq, axis=-1, keepdims=True)
    lo = jnp.sum(jnp.where(lo_mask, sq, 0.0), axis=-1, keepdims=True)
    return jnp.where(lo_mask, lo, tot - lo) * (1.0 / A_QK_DIM)


def _qk_norm_rope(z, gain, ct, su, sd, scale):
    lane = lax.broadcasted_iota(jnp.int32, (1, LANES), 1)
    lo_mask = lane < A_QK_DIM
    outs = []
    for j in range(z.shape[1] // LANES):
        blk = z[:, j * LANES:(j + 1) * LANES]
        ms = _group_mean_square(blk * blk, lo_mask)
        xg = blk * lax.rsqrt(ms + EPS) * gain
        up = pltpu.roll(xg, ROPE_HALF, axis=1)
        dn = pltpu.roll(xg, LANES - ROPE_HALF, axis=1)
        outs.append((xg * ct + up * su + dn * sd) * scale)
    return jnp.concatenate(outs, axis=1)


def _log_sigmoid(x):
    return jnp.minimum(x, 0.0) - jnp.log1p(jnp.exp(-jnp.abs(x)))


def _proj_kernel(x_ref, g_ref, w_ref, wg_ref, gb_ref, qg_ref, kg_ref, ct_ref, su_ref, sd_ref,
                 z_ref, gc_ref, gr_ref, qt_ref, vt_ref):
    x = x_ref[...]
    ms = jnp.mean(x * x, axis=-1, keepdims=True)
    u = (x * lax.rsqrt(ms + EPS) * g_ref[...]).astype(BF16)
    ct, su, sd = ct_ref[...], su_ref[...], sd_ref[...]
    for c in range(N_GROUPS):
        cols = slice(c * M_WIDTH, (c + 1) * M_WIDTH)
        zc = jnp.dot(u, w_ref[:, cols], preferred_element_type=F32)
        if c == COL_AQ:
            zc = _qk_norm_rope(zc, qg_ref[...], ct, su, sd, A_QK_DIM ** -0.5)
        elif c == COL_AK:
            zc = _qk_norm_rope(zc, kg_ref[...], ct, su, sd, 1.0)
        z_ref[:, cols] = zc.astype(BF16)
        if c == COL_AQ:
            qt_ref[...] = zc.T.astype(BF16)
        elif c == COL_AV:
            vt_ref[...] = zc.T.astype(BF16)
    gz = jnp.dot(u, wg_ref[...], preferred_element_type=F32) + gb_ref[...]
    lane = lax.broadcasted_iota(jnp.int32, (1, LANES), 1)
    gz = jnp.where(lane >= M_HEADS, _log_sigmoid(gz), gz)
    gc_ref[...] = gz
    gr_ref[...] = gz.T[:2 * M_HEADS, :]


def _project(x2, attn_norm_g, w_main, w_gate, gate_b, q_gain, k_gain, ct, su, sd, seq):
    tokens = x2.shape[0]
    d_model = x2.shape[1]
    tm = PROJ_ROWS
    pos_blocks = seq // tm
    row_spec = lambda w: pl.BlockSpec((tm, w), lambda i: (i, 0))
    tab_spec = pl.BlockSpec((tm, LANES), lambda i: (i % pos_blocks, 0))
    return pl.pallas_call(
        _proj_kernel,
        grid=(tokens // tm,),
        in_specs=[row_spec(d_model), _const_spec((1, d_model)), _const_spec(w_main.shape),
                  _const_spec(w_gate.shape), _const_spec((1, LANES)), _const_spec((1, LANES)),
                  _const_spec((1, LANES)), tab_spec, tab_spec, tab_spec],
        out_specs=[row_spec(Z_WIDTH), row_spec(LANES),
                   pl.BlockSpec((2 * M_HEADS, tm), lambda i: (0, i)),
                   pl.BlockSpec((A_WIDTH, tm), lambda i: (0, i)),
                   pl.BlockSpec((A_WIDTH, tm), lambda i: (0, i))],
        out_shape=[jax.ShapeDtypeStruct((tokens, Z_WIDTH), BF16),
                   jax.ShapeDtypeStruct((tokens, LANES), F32),
                   jax.ShapeDtypeStruct((2 * M_HEADS, tokens), F32),
                   jax.ShapeDtypeStruct((A_WIDTH, tokens), BF16),
                   jax.ShapeDtypeStruct((A_WIDTH, tokens), BF16)],
        compiler_params=pltpu.CompilerParams(dimension_semantics=("parallel",),
                                             vmem_limit_bytes=VMEM_LIMIT),
        name="proj",
    )(x2, attn_norm_g, w_main, w_gate, gate_b, q_gain, k_gain, ct, su, sd)


def _mlstm_kernel(qk_ref, v_ref, o_ref, gc_ref, gr_ref, cw_ref, cb_ref, ng_ref, y_ref,
                  xbuf_ref, state_ref, m_ref):
    L = MLSTM_CHUNK
    D = M_HEAD_DIM

    @pl.when(pl.program_id(1) == 0)
    def _():
        xbuf_ref[0:CARRY_ROWS, :] = jnp.zeros((CARRY_ROWS, 2 * M_WIDTH), F32)
        state_ref[...] = jnp.zeros_like(state_ref)
        m_ref[...] = jnp.zeros_like(m_ref)

    xbuf_ref[CARRY_ROWS:, :] = qk_ref[...].astype(F32)
    cw = cw_ref[...]
    conv = cb_ref[...] + cw[CONV_WIDTH - 1:CONV_WIDTH, :] * xbuf_ref[CARRY_ROWS:, :]
    for back in range(1, CONV_WIDTH):
        tap = CONV_WIDTH - 1 - back
        conv = conv + cw[tap:tap + 1, :] * xbuf_ref[pl.ds(CARRY_ROWS - back, L), :]
    xbuf_ref[0:CARRY_ROWS, :] = xbuf_ref[L:L + CARRY_ROWS, :]
    qk = conv * jax.nn.sigmoid(conv)

    gc = gc_ref[...]
    gr = gr_ref[...]
    row = lax.broadcasted_iota(jnp.int32, (L, L), 0)
    col = lax.broadcasted_iota(jnp.int32, (L, L), 1)
    causal = col <= row
    lower = causal.astype(F32)
    upper = (row <= col).astype(F32)
    bc_all = jnp.dot(lower, gc, preferred_element_type=F32, precision=lax.Precision.HIGHEST)
    br_all = jnp.dot(gr, upper, preferred_element_type=F32, precision=lax.Precision.HIGHEST)
    lane_aug = lax.broadcasted_iota(jnp.int32, (L, D), 1)
    ones_col = jnp.where(lane_aug == 0, 1.0, 0.0).astype(BF16)

    for h in range(M_HEADS):
        hs = slice(h * D, (h + 1) * D)
        q = (qk[:, hs] * (D ** -0.5)).astype(BF16)
        k = qk[:, M_WIDTH + h * D:M_WIDTH + (h + 1) * D]
        v_aug = jnp.concatenate([v_ref[:, hs], ones_col], axis=1)
        li_c = gc[:, h:h + 1]
        b_c = bc_all[:, M_HEADS + h:M_HEADS + h + 1]
        li_r = gr[h:h + 1, :]
        b_r = br_all[M_HEADS + h:M_HEADS + h + 1, :]
        m_prev = m_ref[h:h + 1, 0:1]
        c_aug = state_ref[h]

        dlog = jnp.where(causal, b_c - b_r + li_r, NEG_BIG)
        inter = b_c + m_prev
        m_t = jnp.maximum(inter, jnp.max(dlog, axis=-1, keepdims=True))
        w_inter = jnp.exp(inter - m_t)
        s = lax.dot_general(q, k.astype(BF16), (((1,), (1,)), ((), ())),
                            preferred_element_type=F32)
        smat = (s * jnp.exp(dlog - m_t)).astype(BF16)
        tot = w_inter * jnp.dot(q, c_aug.astype(BF16), preferred_element_type=F32) \
            + jnp.dot(smat, v_aug, preferred_element_type=F32)
        den = tot[:, D:D + 1]
        hh = tot[:, :D] / jnp.maximum(jnp.abs(den), jnp.exp(-m_t))

        b_last = b_c[L - 1:L, :]
        g_c = b_last - b_c + li_c
        m_new = jnp.maximum(b_last + m_prev, jnp.max(g_c, axis=0, keepdims=True))
        a_c = jnp.exp(g_c - m_new)
        decay = jnp.exp(b_last + m_prev - m_new)
        k_scaled = (k * a_c).astype(BF16)
        state_ref[h] = decay * c_aug + lax.dot_general(
            k_scaled, v_aug, (((0,), (0,)), ((), ())), preferred_element_type=F32)
        m_ref[h:h + 1, :] = jnp.broadcast_to(m_new, (1, LANES))

        hn = hh * lax.rsqrt(jnp.mean(hh * hh, axis=-1, keepdims=True) + EPS) * ng_ref[:, hs]
        y_ref[:, hs] = (jax.nn.sigmoid(o_ref[:, hs].astype(F32)) * hn).astype(BF16)


def _mlstm(z, gates_c, gates_r, conv_w, conv_b, norm_g, batch, seq):
    L = MLSTM_CHUNK
    nt = seq // L
    tokens = batch * seq
    blk = M_WIDTH
    return pl.pallas_call(
        _mlstm_kernel,
        grid=(batch, nt),
        in_specs=[pl.BlockSpec((L, 2 * blk), lambda b, t: (b * nt + t, 0)),
                  pl.BlockSpec((L, blk), lambda b, t: (b * nt + t, COL_MV)),
                  pl.BlockSpec((L, blk), lambda b, t: (b * nt + t, COL_MO)),
                  pl.BlockSpec((L, LANES), lambda b, t: (b * nt + t, 0)),
                  pl.BlockSpec((2 * M_HEADS, L), lambda b, t: (0, b * nt + t)),
                  _const_spec((CONV_WIDTH, 2 * blk)), _const_spec((1, 2 * blk)),
                  _const_spec((1, blk))],
        out_specs=pl.BlockSpec((L, blk), lambda b, t: (b * nt + t, 0)),
        out_shape=jax.ShapeDtypeStruct((tokens, blk), BF16),
        scratch_shapes=[pltpu.VMEM((L + CARRY_ROWS, 2 * blk), F32),
                        pltpu.VMEM((M_HEADS, M_HEAD_DIM, 2 * M_HEAD_DIM), F32),
                        pltpu.VMEM((2 * M_HEADS, LANES), F32)],
        compiler_params=pltpu.CompilerParams(dimension_semantics=("parallel", "arbitrary"),
                                             vmem_limit_bytes=VMEM_LIMIT),
        name="mlstm",
    )(z, z, z, gates_c, gates_r, conv_w, conv_b, norm_g)


ONES_ROWS = 16


def _attn_kernel(qt_ref, k_ref, vt_ref, lam_ref, sg_ref, y_ref, m_ref, acc_ref, s0_ref, s1_ref):
    TQ = ATTN_Q_BLOCK
    TK = ATTN_K_BLOCK
    D = A_HEAD_DIM
    i = pl.program_id(1)
    sub = lax.broadcasted_iota(jnp.int32, (D, 1), 0)
    qz = []
    for h in range(A_HEADS):
        qt = qt_ref[h * D:(h + 1) * D, :]
        zero = jnp.zeros_like(qt)
        qz.append(jnp.concatenate([jnp.where(sub < A_QK_DIM, qt, zero),
                                   jnp.where(sub >= A_QK_DIM, qt, zero)], axis=1))
    m_ref[...] = jnp.full_like(m_ref, NEG_BIG)
    acc_ref[...] = jnp.zeros_like(acc_ref)
    ones = jnp.ones((ONES_ROWS, TK), BF16)

    def score_tile(t, s_ref):
        start = pl.multiple_of(t * TK, TK)
        for h in range(A_HEADS):
            k = k_ref[pl.ds(start, TK), h * D:(h + 1) * D]
            s_ref[h] = jnp.dot(k, qz[h], preferred_element_type=F32)

    def softmax_tile(t, s_ref, diag_offset):
        start = pl.multiple_of(t * TK, TK)
        probs, alphas = [], []
        for h in range(A_HEADS):
            s = s_ref[h]
            if diag_offset is not None:
                r = lax.broadcasted_iota(jnp.int32, (TK, 2 * TQ), 0) + diag_offset
                c = lax.broadcasted_iota(jnp.int32, (TK, 2 * TQ), 1)
                s = jnp.where(r <= jnp.where(c >= TQ, c - TQ, c), s, NEG_BIG)
            m_old = m_ref[h:h + 1, :]
            m_new = jnp.maximum(m_old, jnp.max(s, axis=0, keepdims=True))
            alphas.append(jnp.exp(m_old - m_new))
            probs.append(jnp.exp(s - m_new).astype(BF16))
            m_ref[h:h + 1, :] = m_new
        for h in range(A_HEADS):
            vt = jnp.concatenate([vt_ref[h * D:(h + 1) * D, pl.ds(start, TK)], ones], axis=0)
            acc_ref[h] = alphas[h] * acc_ref[h] + jnp.dot(vt, probs[h],
                                                           preferred_element_type=F32)

    score_tile(0, s0_ref)

    def body(jj, carry):
        score_tile(2 * jj + 1, s1_ref)
        softmax_tile(2 * jj, s0_ref, None)
        score_tile(2 * jj + 2, s0_ref)
        softmax_tile(2 * jj + 1, s1_ref, None)
        return carry

    lax.fori_loop(0, i, body, 0)
    score_tile(2 * i + 1, s1_ref)
    softmax_tile(2 * i, s0_ref, 0)
    softmax_tile(2 * i + 1, s1_ref, TK)

    lq1, lk1, lq2, lk2 = lam_ref[0:1, :], lam_ref[1:2, :], lam_ref[2:3, :], lam_ref[3:4, :]
    lam = jnp.exp(jnp.sum(lq1 * lk1, axis=-1, keepdims=True)) \
        - jnp.exp(jnp.sum(lq2 * lk2, axis=-1, keepdims=True)) + LAMBDA_INIT
    for h in range(A_HEADS):
        hs = slice(h * D, (h + 1) * D)
        acc = acc_ref[h]
        ot = acc[:D, :] / acc[D:D + 1, :]
        o = (ot[:, :TQ] - lam * ot[:, TQ:]).T
        on = o * lax.rsqrt(jnp.mean(o * o, axis=-1, keepdims=True) + EPS) * sg_ref[:, hs]
        y_ref[:, hs] = (on * (1.0 - LAMBDA_INIT)).astype(BF16)


def _attention(z, q_t, v_t, lam_vecs, sub_g, batch, seq):
    TQ = ATTN_Q_BLOCK
    nq = seq // TQ
    tokens = batch * seq
    score_buf = pltpu.VMEM((A_HEADS, ATTN_K_BLOCK, 2 * TQ), F32)
    return pl.pallas_call(
        _attn_kernel,
        grid=(batch, nq),
        in_specs=[pl.BlockSpec((A_WIDTH, TQ), lambda b, i: (0, b * nq + i)),
                  pl.BlockSpec((seq, A_WIDTH), lambda b, i: (b, COL_AK)),
                  pl.BlockSpec((A_WIDTH, seq), lambda b, i: (0, b)),
                  _const_spec((4, LANES)), _const_spec((1, A_WIDTH))],
        out_specs=pl.BlockSpec((TQ, A_WIDTH), lambda b, i: (b * nq + i, 0)),
        out_shape=jax.ShapeDtypeStruct((tokens, A_WIDTH), BF16),
        scratch_shapes=[pltpu.VMEM((2 * A_HEADS, 2 * TQ), F32),
                        pltpu.VMEM((A_HEADS, A_HEAD_DIM + ONES_ROWS, 2 * TQ), F32),
                        score_buf, score_buf],
        compiler_params=pltpu.CompilerParams(dimension_semantics=("parallel", "arbitrary"),
                                             vmem_limit_bytes=VMEM_LIMIT),
        name="diff_attn",
    )(q_t, z, v_t, lam_vecs, sub_g)


def _rms(x, g):
    return x * lax.rsqrt(jnp.mean(x * x, axis=-1, keepdims=True) + EPS) * g


def _ffn_kernel(x_ref, ym_ref, ya_ref, p_ref, wo_ref, g2_ref, wu_ref, wd_ref, g3_ref, wg_ref,
                wp_ref, out_ref):
    h = x_ref[...] \
        + jnp.dot(ym_ref[...], wo_ref[0:M_WIDTH, :], preferred_element_type=F32) \
        + jnp.dot(ya_ref[...], wo_ref[M_WIDTH:, :], preferred_element_type=F32)
    u2 = _rms(h, g2_ref[...]).astype(BF16)
    mlp = None
    for c in range(wu_ref.shape[1] // FF_CHUNK):
        cols = slice(c * FF_CHUNK, (c + 1) * FF_CHUNK)
        hid = jnp.maximum(jnp.dot(u2, wu_ref[:, cols], preferred_element_type=F32), 0.0)
        part = jnp.dot((hid * hid).astype(BF16), wd_ref[cols, :], preferred_element_type=F32)
        mlp = part if mlp is None else mlp + part
    h = h + mlp
    u3 = _rms(h, g3_ref[...]).astype(BF16)
    gate = jax.nn.sigmoid(jnp.dot(u3, wg_ref[...], preferred_element_type=F32))
    e = jnp.dot(p_ref[...].astype(BF16), wp_ref[...], preferred_element_type=F32)
    out_ref[...] = h + gate * e


def _ffn(x2, y_m, y_a, p2, w_out, g2, w_up, w_down, g3, w_gate, w_ple):
    tokens, d_model = x2.shape
    tm = FFN_ROWS
    row_spec = lambda w: pl.BlockSpec((tm, w), lambda i: (i, 0))
    return pl.pallas_call(
        _ffn_kernel,
        grid=(tokens // tm,),
        in_specs=[row_spec(d_model), row_spec(M_WIDTH), row_spec(A_WIDTH), row_spec(p2.shape[1]),
                  _const_spec(w_out.shape), _const_spec((1, d_model)), _const_spec(w_up.shape),
                  _const_spec(w_down.shape), _const_spec((1, d_model)),
                  _const_spec(w_gate.shape), _const_spec(w_ple.shape)],
        out_specs=row_spec(d_model),
        out_shape=jax.ShapeDtypeStruct((tokens, d_model), F32),
        compiler_params=pltpu.CompilerParams(dimension_semantics=("parallel",),
                                             vmem_limit_bytes=VMEM_LIMIT),
        name="ffn",
    )(x2, y_m, y_a, p2, w_out, g2, w_up, w_down, g3, w_gate, w_ple)


def _rope_tables(seq):
    pos = jnp.arange(seq, dtype=F32)
    inv_freq = ROPE_THETA ** (-jnp.arange(0, ROPE_DIM, 2, dtype=F32) / ROPE_DIM)
    ang = pos[:, None] * inv_freq[None, :]
    cos, sin = jnp.cos(ang), jnp.sin(ang)
    pad = jnp.zeros((seq, A_QK_DIM - ROPE_DIM), F32)
    zero8 = jnp.zeros((seq, ROPE_HALF), F32)
    ct = jnp.concatenate([cos, cos, pad + 1.0], axis=1)
    su = jnp.concatenate([zero8, sin, pad], axis=1)
    sd = jnp.concatenate([-sin, zero8, pad], axis=1)
    rep = LANES // A_QK_DIM
    return tuple(jnp.tile(t, (1, rep)) for t in (ct, su, sd))


def kernel(x, p, attn_norm_g, w_in, conv_w, conv_b, igate_b, fgate_b, mlstm_norm_g, q_norm_g,
           k_norm_g, lambda_q1, lambda_k1, lambda_q2, lambda_k2, attn_sub_norm_g, w_out,
           mlp_norm_g, w_up, w_down, ple_norm_g, w_ple_gate, w_ple_proj):
    batch, seq, d_model = x.shape
    assert w_in.shape[0] == 1, "single-layer block"
    assert seq % PROJ_ROWS == 0 and seq % MLSTM_CHUNK == 0 and seq % ATTN_Q_BLOCK == 0
    tokens = batch * seq
    x2 = x.reshape(tokens, d_model).astype(F32)
    p2 = p[0].reshape(tokens, p.shape[-1])

    w = w_in[0]
    gate_lo = 4 * M_WIDTH
    gate_hi = gate_lo + 2 * M_HEADS
    w_main = jnp.concatenate([w[:, :gate_lo], w[:, gate_hi:]], axis=1).astype(BF16)
    w_gatecols = jnp.pad(w[:, gate_lo:gate_hi], ((0, 0), (0, LANES - 2 * M_HEADS))).astype(BF16)
    gate_b = jnp.pad(jnp.concatenate([igate_b[0], fgate_b[0]]),
                     (0, LANES - 2 * M_HEADS)).reshape(1, LANES).astype(F32)
    rep = LANES // A_QK_DIM
    q_gain = jnp.tile(q_norm_g[0].astype(F32), rep).reshape(1, LANES)
    k_gain = jnp.tile(k_norm_g[0].astype(F32), rep).reshape(1, LANES)
    ct, su, sd = _rope_tables(seq)

    z, gates_c, gates_r, q_t, v_t = _project(x2, attn_norm_g[0].reshape(1, d_model).astype(F32), w_main,
                                   w_gatecols, gate_b, q_gain, k_gain, ct, su, sd, seq)

    y_m = _mlstm(z, gates_c, gates_r, conv_w[0].astype(F32),
                 conv_b[0].reshape(1, -1).astype(F32),
                 mlstm_norm_g[0].reshape(1, -1).astype(F32), batch, seq)

    lam_vecs = jnp.pad(jnp.stack([lambda_q1[0], lambda_k1[0], lambda_q2[0], lambda_k2[0]]),
                       ((0, 0), (0, LANES - A_QK_DIM))).astype(F32)
    y_a = _attention(z, q_t, v_t, lam_vecs, attn_sub_norm_g[0].reshape(1, -1).astype(F32), batch, seq)

    out = _ffn(x2, y_m, y_a, p2, w_out[0].astype(BF16),
               mlp_norm_g[0].reshape(1, d_model).astype(F32), w_up[0].astype(BF16),
               w_down[0].astype(BF16), ple_norm_g[0].reshape(1, d_model).astype(F32),
               w_ple_gate[0].astype(BF16), w_ple_proj[0].astype(BF16))
    return out.reshape(batch, seq, d_model).astype(x.dtype)
```

```python
import functools
import math

import jax
import jax.numpy as jnp
from jax import lax
from jax.experimental import pallas as pl
from jax.experimental.pallas import tpu as pltpu

F32 = jnp.float32
BF16 = jnp.bfloat16

EPS = 1e-6
LANES = 128
M_HEADS = 4
M_HEAD_DIM = 128
M_WIDTH = M_HEADS * M_HEAD_DIM
CONV_WIDTH = 4
A_HEADS = 4
A_HEAD_DIM = 128
A_QK_DIM = 64
A_WIDTH = A_HEADS * A_HEAD_DIM
ROPE_THETA = 500000.0
ROPE_DIM = A_QK_DIM // 4
ROPE_HALF = ROPE_DIM // 2
LAMBDA_INIT = 0.8 - 0.6 * math.exp(-0.3 * 0)
LOG2E = math.log2(math.e)

COL_MQ, COL_MK, COL_MV, COL_MO, COL_AQ, COL_AK, COL_AV = range(7)
N_GROUPS = 7
Z_WIDTH = N_GROUPS * M_WIDTH

PROJ_ROWS = 512
MLSTM_CHUNK = 256
ATTN_K_BLOCK = 256
ATTN_Q_BLOCK = 2 * ATTN_K_BLOCK
FFN_ROWS = 512
FF_CHUNK = 1024
ONES_ROWS = 16
VT_ROWS = A_HEAD_DIM + ONES_ROWS
CARRY_ROWS = 8
NEG_BIG = -2.0 ** 100
VMEM_LIMIT = 56 * 1024 * 1024


def _const_spec(shape):
    return pl.BlockSpec(shape, lambda *_: (0,) * len(shape), pipeline_mode=pl.Buffered(1))


def _group_mean_square(sq, lo_mask):
    tot = jnp.sum(sq, axis=-1, keepdims=True)
    lo = jnp.sum(jnp.where(lo_mask, sq, 0.0), axis=-1, keepdims=True)
    return jnp.where(lo_mask, lo, tot - lo) * (1.0 / A_QK_DIM)


def _qk_norm_rope(z, gain, ct, su, sd, scale):
    lane = lax.broadcasted_iota(jnp.int32, (1, LANES), 1)
    lo_mask = lane < A_QK_DIM
    outs = []
    for j in range(z.shape[1] // LANES):
        blk = z[:, j * LANES:(j + 1) * LANES]
        ms = _group_mean_square(blk * blk, lo_mask)
        xg = blk * lax.rsqrt(ms + EPS) * gain
        up = pltpu.roll(xg, ROPE_HALF, axis=1)
        dn = pltpu.roll(xg, LANES - ROPE_HALF, axis=1)
        outs.append((xg * ct + up * su + dn * sd) * scale)
    return jnp.concatenate(outs, axis=1)


def _log_sigmoid(x):
    return jnp.minimum(x, 0.0) - jnp.log1p(jnp.exp(-jnp.abs(x)))


def _conv_silu(xbuf_ref, cols, rows, cw, cb):
    conv = cb + cw[CONV_WIDTH - 1:CONV_WIDTH, :] * xbuf_ref[CARRY_ROWS:, cols]
    for back in range(1, CONV_WIDTH):
        tap = CONV_WIDTH - 1 - back
        conv = conv + cw[tap:tap + 1, :] * xbuf_ref[pl.ds(CARRY_ROWS - back, rows), cols]
    xbuf_ref[0:CARRY_ROWS, cols] = xbuf_ref[rows:rows + CARRY_ROWS, cols]
    return conv * jax.nn.sigmoid(conv)


PROJ_ORDER = (COL_AQ, COL_AK, COL_AV, COL_MQ, COL_MK, COL_MV, COL_MO)
STAGED = {COL_MQ: None, COL_MK: None, COL_AQ: 0, COL_AK: 1, COL_AV: 0}


def _proj_kernel(x_ref, g_ref, w_ref, wg_ref, gb_ref, qg_ref, kg_ref, ct_ref, su_ref, sd_ref,
                 cw_ref, cb_ref, z_ref, gc_ref, gr_ref, qt_ref, vt_ref, kt_ref, xbuf_ref, zbuf_ref, *,
                 pos_blocks):
    rows = x_ref.shape[0]

    @pl.when(pl.program_id(0) % pos_blocks == 0)
    def _():
        xbuf_ref[0:CARRY_ROWS, :] = jnp.zeros((CARRY_ROWS, 2 * M_WIDTH), F32)

    x = x_ref[...]
    ms = jnp.mean(x * x, axis=-1, keepdims=True)
    u = (x * lax.rsqrt(ms + EPS) * g_ref[...]).astype(BF16)

    def stage(c):
        cols = slice(c * M_WIDTH, (c + 1) * M_WIDTH)
        zc = jnp.dot(u, w_ref[:, cols], preferred_element_type=F32)
        if c in (COL_MQ, COL_MK):
            xbuf_ref[CARRY_ROWS:, cols] = zc
        elif c in STAGED:
            zbuf_ref[STAGED[c]] = zc
        else:
            z_ref[:, cols] = zc.astype(BF16)

    def epilogue(c):
        cols = slice(c * M_WIDTH, (c + 1) * M_WIDTH)
        if c == COL_MQ:
            zc = _conv_silu(xbuf_ref, cols, rows, cw_ref[:, cols], cb_ref[:, cols]) \
                * M_HEAD_DIM ** -0.5
        elif c == COL_MK:
            zc = _conv_silu(xbuf_ref, cols, rows, cw_ref[:, cols], cb_ref[:, cols])
        elif c == COL_AQ:
            zc = _qk_norm_rope(zbuf_ref[STAGED[c]], qg_ref[...], ct_ref[...], su_ref[...],
                               sd_ref[...], A_QK_DIM ** -0.5 * LOG2E)
        elif c == COL_AK:
            zc = _qk_norm_rope(zbuf_ref[STAGED[c]], kg_ref[...], ct_ref[...], su_ref[...],
                               sd_ref[...], 1.0)
        elif c == COL_AV:
            zc = zbuf_ref[STAGED[c]]
        else:
            return
        z_ref[:, cols] = zc.astype(BF16)
        if c == COL_MK:
            kt_ref[...] = zc.T.astype(BF16)
        elif c == COL_AQ:
            qt_ref[...] = zc.T.astype(BF16)
        elif c == COL_AV:
            zt = zc.T.astype(BF16)
            for h in range(A_HEADS):
                vt_ref[h * VT_ROWS:h * VT_ROWS + A_HEAD_DIM, :] = \
                    zt[h * A_HEAD_DIM:(h + 1) * A_HEAD_DIM, :]
                vt_ref[h * VT_ROWS + A_HEAD_DIM:(h + 1) * VT_ROWS, :] = \
                    jnp.ones((ONES_ROWS, rows), BF16)

    stage(PROJ_ORDER[0])
    for prev, cur in zip(PROJ_ORDER[:-1], PROJ_ORDER[1:]):
        stage(cur)
        epilogue(prev)
    gz = jnp.dot(u, wg_ref[...], preferred_element_type=F32) + gb_ref[...]
    epilogue(PROJ_ORDER[-1])
    lane = lax.broadcasted_iota(jnp.int32, (1, LANES), 1)
    gz = jnp.where(lane >= M_HEADS, _log_sigmoid(gz), gz)
    gc_ref[...] = gz
    gr_ref[...] = gz.T[:2 * M_HEADS, :]


def _project(x2, attn_norm_g, w_main, w_gate, gate_b, q_gain, k_gain, ct, su, sd, conv_w, conv_b,
             seq):
    tokens = x2.shape[0]
    d_model = x2.shape[1]
    tm = PROJ_ROWS
    pos_blocks = seq // tm
    row_spec = lambda w: pl.BlockSpec((tm, w), lambda i: (i, 0))
    tab_spec = pl.BlockSpec((tm, LANES), lambda i: (i % pos_blocks, 0))
    return pl.pallas_call(
        functools.partial(_proj_kernel, pos_blocks=pos_blocks),
        grid=(tokens // tm,),
        in_specs=[row_spec(d_model), _const_spec((1, d_model)), _const_spec(w_main.shape),
                  _const_spec(w_gate.shape), _const_spec((1, LANES)), _const_spec((1, LANES)),
                  _const_spec((1, LANES)), tab_spec, tab_spec, tab_spec,
                  _const_spec((CONV_WIDTH, 2 * M_WIDTH)), _const_spec((1, 2 * M_WIDTH))],
        out_specs=[row_spec(Z_WIDTH), row_spec(LANES),
                   pl.BlockSpec((2 * M_HEADS, tm), lambda i: (0, i)),
                   pl.BlockSpec((A_WIDTH, tm), lambda i: (0, i)),
                   pl.BlockSpec((A_HEADS * VT_ROWS, tm), lambda i: (0, i)),
                   pl.BlockSpec((M_WIDTH, tm), lambda i: (0, i))],
        out_shape=[jax.ShapeDtypeStruct((tokens, Z_WIDTH), BF16),
                   jax.ShapeDtypeStruct((tokens, LANES), F32),
                   jax.ShapeDtypeStruct((2 * M_HEADS, tokens), F32),
                   jax.ShapeDtypeStruct((A_WIDTH, tokens), BF16),
                   jax.ShapeDtypeStruct((A_HEADS * VT_ROWS, tokens), BF16),
                   jax.ShapeDtypeStruct((M_WIDTH, tokens), BF16)],
        scratch_shapes=[pltpu.VMEM((tm + CARRY_ROWS, 2 * M_WIDTH), F32),
                        pltpu.VMEM((2, tm, M_WIDTH), F32)],
        compiler_params=pltpu.CompilerParams(dimension_semantics=("arbitrary",),
                                             vmem_limit_bytes=VMEM_LIMIT),
        name="proj",
    )(x2, attn_norm_g, w_main, w_gate, gate_b, q_gain, k_gain, ct, su, sd, conv_w, conv_b)


def _mlstm_kernel(q_ref, kt_ref, v_ref, o_ref, gc_ref, gr_ref, ng_ref, y_ref, state_ref, m_ref):
    L = MLSTM_CHUNK
    D = M_HEAD_DIM
    H = M_HEADS

    @pl.when(pl.program_id(1) == 0)
    def _():
        state_ref[...] = jnp.zeros_like(state_ref)
        m_ref[...] = jnp.zeros_like(m_ref)

    gc = gc_ref[...]
    gr = gr_ref[...]
    row = lax.broadcasted_iota(jnp.int32, (L, L), 0)
    col = lax.broadcasted_iota(jnp.int32, (L, L), 1)
    causal = col <= row
    lower = causal.astype(F32)
    upper = (row <= col).astype(F32)
    bc_all = jnp.dot(lower, gc, preferred_element_type=F32, precision=lax.Precision.HIGHEST)
    br_all = jnp.dot(gr, upper, preferred_element_type=F32, precision=lax.Precision.HIGHEST)
    ones_blk = jnp.ones((L, D), BF16)
    mean_mat = jnp.full((D, D), 1.0 / D, BF16)

    q = [q_ref[:, h * D:(h + 1) * D] for h in range(H)]
    v_aug = [jnp.concatenate([v_ref[:, h * D:(h + 1) * D], ones_blk], axis=1) for h in range(H)]
    m_prev = [m_ref[h:h + 1, 0:1] for h in range(H)]
    b_c = [bc_all[:, H + h:H + h + 1] for h in range(H)]
    b_r = [br_all[H + h:H + h + 1, :] for h in range(H)]
    li_r = [gr[h:h + 1, :] for h in range(H)]
    c_state = [state_ref[h] for h in range(H)]

    s = [jnp.dot(q[h], kt_ref[h * D:(h + 1) * D, :], preferred_element_type=F32)
         for h in range(H)]
    inter = [jnp.dot(q[h], c_state[h].astype(BF16), preferred_element_type=F32)
             for h in range(H)]

    b_last = [b_r[h][:, L - 1:L] for h in range(H)]
    g_r = [b_last[h] - b_r[h] + li_r[h] for h in range(H)]
    m_new = [jnp.maximum(b_last[h] + m_prev[h], jnp.max(g_r[h], axis=-1, keepdims=True))
             for h in range(H)]
    kt_scaled = [(kt_ref[h * D:(h + 1) * D, :].astype(F32)
                  * jnp.exp(g_r[h] - m_new[h])).astype(BF16) for h in range(H)]
    for h in range(H):
        decay = jnp.exp(b_last[h] + m_prev[h] - m_new[h])
        state_ref[h] = decay * c_state[h] + jnp.dot(kt_scaled[h], v_aug[h],
                                                    preferred_element_type=F32)
        m_ref[h:h + 1, :] = jnp.broadcast_to(m_new[h], (1, LANES))

    e = [jnp.where(causal, li_r[h] - b_r[h], NEG_BIG) for h in range(H)]
    u = [jnp.maximum(m_prev[h], jnp.max(e[h], axis=-1, keepdims=True)) for h in range(H)]
    u_rep = [jnp.broadcast_to(u[h], (L, D)) for h in range(H)]
    smat = [(s[h] * jnp.exp(e[h] - jnp.concatenate([u_rep[h]] * (L // D), axis=1))).astype(BF16)
            for h in range(H)]
    intra = [jnp.dot(smat[h], v_aug[h], preferred_element_type=F32) for h in range(H)]

    w_rep = [jnp.exp(m_prev[h] - u_rep[h]) for h in range(H)]
    num = [w_rep[h] * inter[h][:, :D] + intra[h][:, :D] for h in range(H)]
    den = [w_rep[h] * inter[h][:, D:] + intra[h][:, D:] for h in range(H)]
    floor_rep = [jnp.broadcast_to(jnp.exp(-(b_c[h] + u[h])), (L, D)) for h in range(H)]
    msq = [jnp.dot((num[h] * num[h]).astype(BF16), mean_mat, preferred_element_type=F32)
           for h in range(H)]
    for h in range(H):
        hs = slice(h * D, (h + 1) * D)
        r = 1.0 / jnp.maximum(jnp.abs(den[h]), floor_rep[h])
        f = r * lax.rsqrt(r * r * msq[h] + EPS)
        y_ref[:, hs] = (jax.nn.sigmoid(o_ref[:, hs].astype(F32)) * (num[h] * f)
                        * ng_ref[:, hs]).astype(BF16)


def _mlstm(z, k_t, gates_c, gates_r, norm_g, batch, seq):
    L = MLSTM_CHUNK
    nt = seq // L
    tokens = batch * seq
    blk = M_WIDTH
    return pl.pallas_call(
        _mlstm_kernel,
        grid=(batch, nt),
        in_specs=[pl.BlockSpec((L, blk), lambda b, t: (b * nt + t, COL_MQ)),
                  pl.BlockSpec((blk, L), lambda b, t: (0, b * nt + t)),
                  pl.BlockSpec((L, blk), lambda b, t: (b * nt + t, COL_MV)),
                  pl.BlockSpec((L, blk), lambda b, t: (b * nt + t, COL_MO)),
                  pl.BlockSpec((L, LANES), lambda b, t: (b * nt + t, 0)),
                  pl.BlockSpec((2 * M_HEADS, L), lambda b, t: (0, b * nt + t)),
                  _const_spec((1, blk))],
        out_specs=pl.BlockSpec((L, blk), lambda b, t: (b * nt + t, 0)),
        out_shape=jax.ShapeDtypeStruct((tokens, blk), BF16),
        scratch_shapes=[pltpu.VMEM((M_HEADS, M_HEAD_DIM, 2 * M_HEAD_DIM), F32),
                        pltpu.VMEM((2 * M_HEADS, LANES), F32)],
        compiler_params=pltpu.CompilerParams(dimension_semantics=("parallel", "arbitrary"),
                                             vmem_limit_bytes=VMEM_LIMIT),
        name="mlstm",
    )(z, k_t, z, z, gates_c, gates_r, norm_g)


def _attn_kernel(qt_ref, k_ref, vt_ref, lam_ref, sg_ref, y_ref, m_ref, acc_ref, s0_ref, s1_ref,
                 qz_ref):
    TQ = ATTN_Q_BLOCK
    TK = ATTN_K_BLOCK
    D = A_HEAD_DIM
    i = pl.program_id(1)
    sub = lax.broadcasted_iota(jnp.int32, (D, 1), 0)
    for h in range(A_HEADS):
        qt = qt_ref[h * D:(h + 1) * D, :]
        zero = jnp.zeros_like(qt)
        qz_ref[h, :, 0:TQ] = jnp.where(sub < A_QK_DIM, qt, zero)
        qz_ref[h, :, TQ:2 * TQ] = jnp.where(sub >= A_QK_DIM, qt, zero)
    m_ref[...] = jnp.full_like(m_ref, NEG_BIG)
    acc_ref[...] = jnp.zeros_like(acc_ref)

    hi_cols = (slice(TK, TQ), slice(TQ + TK, 2 * TQ))

    def score_tile(t, s_ref):
        start = pl.multiple_of(t * TK, TK)
        for h in range(A_HEADS):
            k = k_ref[pl.ds(start, TK), h * D:(h + 1) * D]
            s_ref[h] = jnp.dot(k, qz_ref[h], preferred_element_type=F32).astype(BF16)

    def softmax_tile(t, scores, cols=None):
        start = pl.multiple_of(t * TK, TK)
        probs, alphas = [], []
        for h in range(A_HEADS):
            s = scores(h)
            m_old = m_ref[h:h + 1, :] if cols is None else \
                jnp.concatenate([m_ref[h:h + 1, c] for c in cols], axis=1)
            m_new = jnp.maximum(m_old, jnp.max(s, axis=0, keepdims=True).astype(F32))
            alphas.append(jnp.exp2(m_old - m_new))
            probs.append(jnp.exp2(s - m_new.astype(BF16)))
            if cols is None:
                m_ref[h:h + 1, :] = m_new
            else:
                for n, c in enumerate(cols):
                    m_ref[h:h + 1, c] = m_new[:, n * TK:(n + 1) * TK]
        for h in range(A_HEADS):
            vt = vt_ref[h * VT_ROWS:(h + 1) * VT_ROWS, pl.ds(start, TK)]
            pv = jnp.dot(vt, probs[h], preferred_element_type=F32)
            if cols is None:
                acc_ref[h] = pv + alphas[h] * acc_ref[h]
            else:
                for n, c in enumerate(cols):
                    blk = slice(n * TK, (n + 1) * TK)
                    acc_ref[h, :, c] = alphas[h][:, blk] * acc_ref[h, :, c] + pv[:, blk]

    score_tile(0, s0_ref)

    def body(jj, carry):
        score_tile(2 * jj + 1, s1_ref)
        softmax_tile(2 * jj, lambda h: s0_ref[h])
        score_tile(2 * jj + 2, s0_ref)
        softmax_tile(2 * jj + 1, lambda h: s1_ref[h])
        return carry

    lax.fori_loop(0, i, body, 0)

    visible = lax.broadcasted_iota(jnp.int32, (TK, TK), 0) <= \
        lax.broadcasted_iota(jnp.int32, (TK, TK), 1)
    start_b = pl.multiple_of((2 * i + 1) * TK, TK)
    for h in range(A_HEADS):
        k = k_ref[pl.ds(start_b, TK), h * D:(h + 1) * D]
        for n, c in enumerate(hi_cols):
            s1_ref[h, :, n * TK:(n + 1) * TK] = jnp.dot(
                k, qz_ref[h, :, c], preferred_element_type=F32).astype(BF16)

    def diag_first(h):
        s = s0_ref[h]
        parts = [s[:, n * TK:(n + 1) * TK] for n in range(2 * TQ // TK)]
        for n in (0, TQ // TK):
            parts[n] = jnp.where(visible, parts[n], NEG_BIG)
        return jnp.concatenate(parts, axis=1)

    def diag_second(h):
        return jnp.concatenate([jnp.where(visible, s1_ref[h, :, n * TK:(n + 1) * TK], NEG_BIG)
                                for n in range(len(hi_cols))], axis=1)

    softmax_tile(2 * i, diag_first)
    softmax_tile(2 * i + 1, diag_second, hi_cols)

    lq1, lk1, lq2, lk2 = lam_ref[0:1, :], lam_ref[1:2, :], lam_ref[2:3, :], lam_ref[3:4, :]
    lam = jnp.exp(jnp.sum(lq1 * lk1, axis=-1, keepdims=True)) \
        - jnp.exp(jnp.sum(lq2 * lk2, axis=-1, keepdims=True)) + LAMBDA_INIT
    for h in range(A_HEADS):
        hs = slice(h * D, (h + 1) * D)
        acc = acc_ref[h]
        ot = acc[:D, :] / acc[D:D + 1, :]
        o = (ot[:, :TQ] - lam * ot[:, TQ:]).T
        on = o * lax.rsqrt(jnp.mean(o * o, axis=-1, keepdims=True) + EPS) * sg_ref[:, hs]
        y_ref[:, hs] = (on * (1.0 - LAMBDA_INIT)).astype(BF16)


def _attention(z, q_t, v_t, lam_vecs, sub_g, batch, seq):
    TQ = ATTN_Q_BLOCK
    nq = seq // TQ
    tokens = batch * seq
    score_buf = pltpu.VMEM((A_HEADS, ATTN_K_BLOCK, 2 * TQ), BF16)
    return pl.pallas_call(
        _attn_kernel,
        grid=(batch, nq),
        in_specs=[pl.BlockSpec((A_WIDTH, TQ), lambda b, i: (0, b * nq + i)),
                  pl.BlockSpec((seq, A_WIDTH), lambda b, i: (b, COL_AK)),
                  pl.BlockSpec((A_HEADS * VT_ROWS, seq), lambda b, i: (0, b)),
                  _const_spec((4, LANES)), _const_spec((1, A_WIDTH))],
        out_specs=pl.BlockSpec((TQ, A_WIDTH), lambda b, i: (b * nq + i, 0)),
        out_shape=jax.ShapeDtypeStruct((tokens, A_WIDTH), BF16),
        scratch_shapes=[pltpu.VMEM((2 * A_HEADS, 2 * TQ), F32),
                        pltpu.VMEM((A_HEADS, VT_ROWS, 2 * TQ), F32),
                        score_buf, score_buf,
                        pltpu.VMEM((A_HEADS, A_HEAD_DIM, 2 * TQ), BF16)],
        compiler_params=pltpu.CompilerParams(dimension_semantics=("parallel", "arbitrary"),
                                             vmem_limit_bytes=VMEM_LIMIT),
        name="diff_attn",
    )(q_t, z, v_t, lam_vecs, sub_g)


def _rms(x, g):
    return x * lax.rsqrt(jnp.mean(x * x, axis=-1, keepdims=True) + EPS) * g


def _ffn_kernel(x_ref, ym_ref, ya_ref, p_ref, wo_ref, g2_ref, wu_ref, wd_ref, g3_ref, wg_ref,
                wp_ref, out_ref):
    h = x_ref[...] \
        + jnp.dot(ym_ref[...], wo_ref[0:M_WIDTH, :], preferred_element_type=F32) \
        + jnp.dot(ya_ref[...], wo_ref[M_WIDTH:, :], preferred_element_type=F32)
    u2 = _rms(h, g2_ref[...]).astype(BF16)
    mlp = None
    for c in range(wu_ref.shape[1] // FF_CHUNK):
        cols = slice(c * FF_CHUNK, (c + 1) * FF_CHUNK)
        hid = jnp.maximum(jnp.dot(u2, wu_ref[:, cols], preferred_element_type=F32), 0.0)
        part = jnp.dot((hid * hid).astype(BF16), wd_ref[cols, :], preferred_element_type=F32)
        mlp = part if mlp is None else mlp + part
    h = h + mlp
    u3 = _rms(h, g3_ref[...]).astype(BF16)
    gate = jax.nn.sigmoid(jnp.dot(u3, wg_ref[...], preferred_element_type=F32))
    e = jnp.dot(p_ref[...].astype(BF16), wp_ref[...], preferred_element_type=F32)
    out_ref[...] = h + gate * e


def _ffn(x2, y_m, y_a, p2, w_out, g2, w_up, w_down, g3, w_gate, w_ple):
    tokens, d_model = x2.shape
    tm = FFN_ROWS
    row_spec = lambda w: pl.BlockSpec((tm, w), lambda i: (i, 0))
    return pl.pallas_call(
        _ffn_kernel,
        grid=(tokens // tm,),
        in_specs=[row_spec(d_model), row_spec(M_WIDTH), row_spec(A_WIDTH), row_spec(p2.shape[1]),
                  _const_spec(w_out.shape), _const_spec((1, d_model)), _const_spec(w_up.shape),
                  _const_spec(w_down.shape), _const_spec((1, d_model)),
                  _const_spec(w_gate.shape), _const_spec(w_ple.shape)],
        out_specs=row_spec(d_model),
        out_shape=jax.ShapeDtypeStruct((tokens, d_model), F32),
        compiler_params=pltpu.CompilerParams(dimension_semantics=("parallel",),
                                             vmem_limit_bytes=VMEM_LIMIT),
        name="ffn",
    )(x2, y_m, y_a, p2, w_out, g2, w_up, w_down, g3, w_gate, w_ple)


def _rope_tables(seq):
    pos = jnp.arange(seq, dtype=F32)
    inv_freq = ROPE_THETA ** (-jnp.arange(0, ROPE_DIM, 2, dtype=F32) / ROPE_DIM)
    ang = pos[:, None] * inv_freq[None, :]
    cos, sin = jnp.cos(ang), jnp.sin(ang)
    pad = jnp.zeros((seq, A_QK_DIM - ROPE_DIM), F32)
    zero8 = jnp.zeros((seq, ROPE_HALF), F32)
    ct = jnp.concatenate([cos, cos, pad + 1.0], axis=1)
    su = jnp.concatenate([zero8, sin, pad], axis=1)
    sd = jnp.concatenate([-sin, zero8, pad], axis=1)
    rep = LANES // A_QK_DIM
    return tuple(jnp.tile(t, (1, rep)) for t in (ct, su, sd))


def kernel(x, p, attn_norm_g, w_in, conv_w, conv_b, igate_b, fgate_b, mlstm_norm_g, q_norm_g,
           k_norm_g, lambda_q1, lambda_k1, lambda_q2, lambda_k2, attn_sub_norm_g, w_out,
           mlp_norm_g, w_up, w_down, ple_norm_g, w_ple_gate, w_ple_proj):
    batch, seq, d_model = x.shape
    assert w_in.shape[0] == 1, "single-layer block"
    assert seq % PROJ_ROWS == 0 and seq % MLSTM_CHUNK == 0 and seq % ATTN_Q_BLOCK == 0
    tokens = batch * seq
    x2 = x.reshape(tokens, d_model).astype(F32)
    p2 = p.reshape(tokens, p.shape[-1])

    w = w_in[0]
    gate_lo = 4 * M_WIDTH
    gate_hi = gate_lo + 2 * M_HEADS
    w_main = jnp.concatenate([w[:, :gate_lo], w[:, gate_hi:]], axis=1).astype(BF16)
    w_gatecols = jnp.pad(w[:, gate_lo:gate_hi], ((0, 0), (0, LANES - 2 * M_HEADS))).astype(BF16)
    gate_b = jnp.pad(jnp.concatenate([igate_b[0], fgate_b[0]]),
                     (0, LANES - 2 * M_HEADS)).reshape(1, LANES).astype(F32)
    rep = LANES // A_QK_DIM
    q_gain = jnp.tile(q_norm_g[0].astype(F32), rep).reshape(1, LANES)
    k_gain = jnp.tile(k_norm_g[0].astype(F32), rep).reshape(1, LANES)
    ct, su, sd = _rope_tables(seq)

    z, gates_c, gates_r, q_t, v_t, k_t = _project(
        x2, attn_norm_g[0].reshape(1, d_model).astype(F32), w_main, w_gatecols, gate_b, q_gain,
        k_gain, ct, su, sd, conv_w[0].astype(F32), conv_b[0].reshape(1, -1).astype(F32), seq)

    y_m = _mlstm(z, k_t, gates_c, gates_r, mlstm_norm_g[0].reshape(1, -1).astype(F32), batch, seq)

    lam_vecs = jnp.pad(jnp.stack([lambda_q1[0], lambda_k1[0], lambda_q2[0], lambda_k2[0]]),
                       ((0, 0), (0, LANES - A_QK_DIM))).astype(F32)
    y_a = _attention(z, q_t, v_t, lam_vecs, attn_sub_norm_g[0].reshape(1, -1).astype(F32), batch, seq)

    out = _ffn(x2, y_m, y_a, p2, w_out[0].astype(BF16),
               mlp_norm_g[0].reshape(1, d_model).astype(F32), w_up[0].astype(BF16),
               w_down[0].astype(BF16), ple_norm_g[0].reshape(1, d_model).astype(F32),
               w_ple_gate[0].astype(BF16), w_ple_proj[0].astype(BF16))
    return out.reshape(batch, seq, d_model).astype(x.dtype)
```

```python
import functools
import math

import jax
import jax.numpy as jnp
from jax import lax
from jax.experimental import pallas as pl
from jax.experimental.pallas import tpu as pltpu

F32 = jnp.float32
BF16 = jnp.bfloat16

EPS = 1e-6
LANES = 128
M_HEADS = 4
M_HEAD_DIM = 128
M_WIDTH = M_HEADS * M_HEAD_DIM
CONV_WIDTH = 4
A_HEADS = 4
A_HEAD_DIM = 128
A_QK_DIM = 64
A_WIDTH = A_HEADS * A_HEAD_DIM
ROPE_THETA = 500000.0
ROPE_DIM = A_QK_DIM // 4
ROPE_HALF = ROPE_DIM // 2
LAMBDA_INIT = 0.8 - 0.6 * math.exp(-0.3 * 0)
LOG2E = math.log2(math.e)

COL_MQ, COL_MK, COL_MV, COL_MO, COL_AQ, COL_AK, COL_AV = range(7)
N_GROUPS = 7
Z_WIDTH = N_GROUPS * M_WIDTH

PROJ_ROWS = 1024
MLSTM_CHUNK = 256
ATTN_K_BLOCK = 256
ATTN_Q_BLOCK = 2 * ATTN_K_BLOCK
FFN_ROWS = 1024
FF_CHUNK = 1024
ONES_ROWS = 16
VT_ROWS = A_HEAD_DIM + ONES_ROWS
CARRY_ROWS = 8
NEG_BIG = -2.0 ** 100
VMEM_LIMIT = 56 * 1024 * 1024


def _const_spec(shape):
    return pl.BlockSpec(shape, lambda *_: (0,) * len(shape), pipeline_mode=pl.Buffered(1))


def _group_mean_square(sq, lo_mask):
    tot = jnp.sum(sq, axis=-1, keepdims=True)
    lo = jnp.sum(jnp.where(lo_mask, sq, 0.0), axis=-1, keepdims=True)
    return jnp.where(lo_mask, lo, tot - lo) * (1.0 / A_QK_DIM)


def _qk_norm_rope(z, gain, ct, su, sd, scale):
    lane = lax.broadcasted_iota(jnp.int32, (1, LANES), 1)
    lo_mask = lane < A_QK_DIM
    outs = []
    for j in range(z.shape[1] // LANES):
        blk = z[:, j * LANES:(j + 1) * LANES]
        ms = _group_mean_square(blk * blk, lo_mask)
        xg = blk * lax.rsqrt(ms + EPS) * gain
        up = pltpu.roll(xg, ROPE_HALF, axis=1)
        dn = pltpu.roll(xg, LANES - ROPE_HALF, axis=1)
        outs.append((xg * ct + up * su + dn * sd) * scale)
    return jnp.concatenate(outs, axis=1)


def _log_sigmoid(x):
    return jnp.minimum(x, 0.0) - jnp.log1p(jnp.exp(-jnp.abs(x)))


def _conv_silu(xbuf_ref, cols, rows, cw, cb):
    conv = cb + cw[CONV_WIDTH - 1:CONV_WIDTH, :] * xbuf_ref[CARRY_ROWS:, cols]
    for back in range(1, CONV_WIDTH):
        tap = CONV_WIDTH - 1 - back
        conv = conv + cw[tap:tap + 1, :] * xbuf_ref[pl.ds(CARRY_ROWS - back, rows), cols]
    xbuf_ref[0:CARRY_ROWS, cols] = xbuf_ref[rows:rows + CARRY_ROWS, cols]
    return conv * jax.nn.sigmoid(conv)


PROJ_ORDER = (COL_AQ, COL_AK, COL_AV, COL_MQ, COL_MK, COL_MV, COL_MO)
STAGED = {COL_MQ: None, COL_MK: None, COL_AQ: 0, COL_AK: 1, COL_AV: 0}


def _proj_kernel(x_ref, g_ref, w_ref, wg_ref, gb_ref, qg_ref, kg_ref, ct_ref, su_ref, sd_ref,
                 cw_ref, cb_ref, z_ref, gc_ref, gr_ref, qt_ref, vt_ref, kt_ref, xbuf_ref, zbuf_ref, *,
                 pos_blocks):
    rows = x_ref.shape[0]

    @pl.when(pl.program_id(0) % pos_blocks == 0)
    def _():
        xbuf_ref[0:CARRY_ROWS, :] = jnp.zeros((CARRY_ROWS, 2 * M_WIDTH), F32)

    x = x_ref[...]
    ms = jnp.mean(x * x, axis=-1, keepdims=True)
    u = (x * lax.rsqrt(ms + EPS) * g_ref[...]).astype(BF16)

    def stage(c):
        cols = slice(c * M_WIDTH, (c + 1) * M_WIDTH)
        zc = jnp.dot(u, w_ref[:, cols], preferred_element_type=F32)
        if c in (COL_MQ, COL_MK):
            xbuf_ref[CARRY_ROWS:, cols] = zc
        elif c in STAGED:
            zbuf_ref[STAGED[c]] = zc
        else:
            z_ref[:, cols] = zc.astype(BF16)

    def epilogue(c):
        cols = slice(c * M_WIDTH, (c + 1) * M_WIDTH)
        if c == COL_MQ:
            zc = _conv_silu(xbuf_ref, cols, rows, cw_ref[:, cols], cb_ref[:, cols]) \
                * M_HEAD_DIM ** -0.5
        elif c == COL_MK:
            zc = _conv_silu(xbuf_ref, cols, rows, cw_ref[:, cols], cb_ref[:, cols])
        elif c == COL_AQ:
            zc = _qk_norm_rope(zbuf_ref[STAGED[c]], qg_ref[...], ct_ref[...], su_ref[...],
                               sd_ref[...], A_QK_DIM ** -0.5 * LOG2E)
        elif c == COL_AK:
            zc = _qk_norm_rope(zbuf_ref[STAGED[c]], kg_ref[...], ct_ref[...], su_ref[...],
                               sd_ref[...], 1.0)
        elif c == COL_AV:
            zc = zbuf_ref[STAGED[c]]
        else:
            return
        z_ref[:, cols] = zc.astype(BF16)
        if c == COL_MK:
            kt_ref[...] = zc.T.astype(BF16)
        elif c == COL_AQ:
            qt_ref[...] = zc.T.astype(BF16)
        elif c == COL_AV:
            zt = zc.T.astype(BF16)
            for h in range(A_HEADS):
                vt_ref[h * VT_ROWS:h * VT_ROWS + A_HEAD_DIM, :] = \
                    zt[h * A_HEAD_DIM:(h + 1) * A_HEAD_DIM, :]
                vt_ref[h * VT_ROWS + A_HEAD_DIM:(h + 1) * VT_ROWS, :] = \
                    jnp.ones((ONES_ROWS, rows), BF16)

    stage(PROJ_ORDER[0])
    for prev, cur in zip(PROJ_ORDER[:-1], PROJ_ORDER[1:]):
        stage(cur)
        epilogue(prev)
    gz = jnp.dot(u, wg_ref[...], preferred_element_type=F32) + gb_ref[...]
    epilogue(PROJ_ORDER[-1])
    lane = lax.broadcasted_iota(jnp.int32, (1, LANES), 1)
    gz = jnp.where(lane >= M_HEADS, _log_sigmoid(gz), gz)
    gc_ref[...] = gz
    gr_ref[...] = gz.T[:2 * M_HEADS, :]


def _project(x2, attn_norm_g, w_main, w_gate, gate_b, q_gain, k_gain, ct, su, sd, conv_w, conv_b,
             seq):
    tokens = x2.shape[0]
    d_model = x2.shape[1]
    tm = PROJ_ROWS
    pos_blocks = seq // tm
    row_spec = lambda w: pl.BlockSpec((tm, w), lambda i: (i, 0))
    tab_spec = pl.BlockSpec((tm, LANES), lambda i: (i % pos_blocks, 0))
    return pl.pallas_call(
        functools.partial(_proj_kernel, pos_blocks=pos_blocks),
        grid=(tokens // tm,),
        in_specs=[row_spec(d_model), _const_spec((1, d_model)), _const_spec(w_main.shape),
                  _const_spec(w_gate.shape), _const_spec((1, LANES)), _const_spec((1, LANES)),
                  _const_spec((1, LANES)), tab_spec, tab_spec, tab_spec,
                  _const_spec((CONV_WIDTH, 2 * M_WIDTH)), _const_spec((1, 2 * M_WIDTH))],
        out_specs=[row_spec(Z_WIDTH), row_spec(LANES),
                   pl.BlockSpec((2 * M_HEADS, tm), lambda i: (0, i)),
                   pl.BlockSpec((A_WIDTH, tm), lambda i: (0, i)),
                   pl.BlockSpec((A_HEADS * VT_ROWS, tm), lambda i: (0, i)),
                   pl.BlockSpec((M_WIDTH, tm), lambda i: (0, i))],
        out_shape=[jax.ShapeDtypeStruct((tokens, Z_WIDTH), BF16),
                   jax.ShapeDtypeStruct((tokens, LANES), F32),
                   jax.ShapeDtypeStruct((2 * M_HEADS, tokens), F32),
                   jax.ShapeDtypeStruct((A_WIDTH, tokens), BF16),
                   jax.ShapeDtypeStruct((A_HEADS * VT_ROWS, tokens), BF16),
                   jax.ShapeDtypeStruct((M_WIDTH, tokens), BF16)],
        scratch_shapes=[pltpu.VMEM((tm + CARRY_ROWS, 2 * M_WIDTH), F32),
                        pltpu.VMEM((2, tm, M_WIDTH), F32)],
        compiler_params=pltpu.CompilerParams(dimension_semantics=("arbitrary",),
                                             vmem_limit_bytes=VMEM_LIMIT),
        name="proj",
    )(x2, attn_norm_g, w_main, w_gate, gate_b, q_gain, k_gain, ct, su, sd, conv_w, conv_b)


def _mlstm_kernel(q_ref, kt_ref, v_ref, o_ref, gc_ref, gr_ref, ng_ref, y_ref, state_ref, m_ref):
    L = MLSTM_CHUNK
    D = M_HEAD_DIM
    H = M_HEADS

    @pl.when(pl.program_id(1) == 0)
    def _():
        state_ref[...] = jnp.zeros_like(state_ref)
        m_ref[...] = jnp.zeros_like(m_ref)

    gc = gc_ref[...]
    gr = gr_ref[...]
    row = lax.broadcasted_iota(jnp.int32, (L, L), 0)
    col = lax.broadcasted_iota(jnp.int32, (L, L), 1)
    causal = col <= row
    lower = causal.astype(F32)
    upper = (row <= col).astype(F32)
    bc_all = jnp.dot(lower, gc, preferred_element_type=F32, precision=lax.Precision.HIGHEST)
    br_all = jnp.dot(gr, upper, preferred_element_type=F32, precision=lax.Precision.HIGHEST)
    ones_blk = jnp.ones((L, D), BF16)
    mean_mat = jnp.full((D, D), 1.0 / D, BF16)

    q = [q_ref[:, h * D:(h + 1) * D] for h in range(H)]
    v_aug = [jnp.concatenate([v_ref[:, h * D:(h + 1) * D], ones_blk], axis=1) for h in range(H)]
    m_prev = [m_ref[h:h + 1, 0:1] for h in range(H)]
    b_c = [bc_all[:, H + h:H + h + 1] for h in range(H)]
    b_r = [br_all[H + h:H + h + 1, :] for h in range(H)]
    li_r = [gr[h:h + 1, :] for h in range(H)]
    c_state = [state_ref[h] for h in range(H)]

    s = [jnp.dot(q[h], kt_ref[h * D:(h + 1) * D, :], preferred_element_type=F32)
         for h in range(H)]
    inter = [jnp.dot(q[h], c_state[h].astype(BF16), preferred_element_type=F32)
             for h in range(H)]

    b_last = [b_r[h][:, L - 1:L] for h in range(H)]
    g_r = [b_last[h] - b_r[h] + li_r[h] for h in range(H)]
    m_new = [jnp.maximum(b_last[h] + m_prev[h], jnp.max(g_r[h], axis=-1, keepdims=True))
             for h in range(H)]
    kt_scaled = [(kt_ref[h * D:(h + 1) * D, :].astype(F32)
                  * jnp.exp(g_r[h] - m_new[h])).astype(BF16) for h in range(H)]
    for h in range(H):
        decay = jnp.exp(b_last[h] + m_prev[h] - m_new[h])
        state_ref[h] = decay * c_state[h] + jnp.dot(kt_scaled[h], v_aug[h],
                                                    preferred_element_type=F32)
        m_ref[h:h + 1, :] = jnp.broadcast_to(m_new[h], (1, LANES))

    e = [jnp.where(causal, li_r[h] - b_r[h], NEG_BIG) for h in range(H)]
    u = [jnp.maximum(m_prev[h], jnp.max(e[h], axis=-1, keepdims=True)) for h in range(H)]
    u_rep = [jnp.broadcast_to(u[h], (L, D)) for h in range(H)]
    smat = [(s[h] * jnp.exp(e[h] - jnp.concatenate([u_rep[h]] * (L // D), axis=1))).astype(BF16)
            for h in range(H)]
    intra = [jnp.dot(smat[h], v_aug[h], preferred_element_type=F32) for h in range(H)]

    w_rep = [jnp.exp(m_prev[h] - u_rep[h]) for h in range(H)]
    num = [w_rep[h] * inter[h][:, :D] + intra[h][:, :D] for h in range(H)]
    den = [w_rep[h] * inter[h][:, D:] + intra[h][:, D:] for h in range(H)]
    floor_rep = [jnp.broadcast_to(jnp.exp(-(b_c[h] + u[h])), (L, D)) for h in range(H)]
    msq = [jnp.dot((num[h] * num[h]).astype(BF16), mean_mat, preferred_element_type=F32)
           for h in range(H)]
    for h in range(H):
        hs = slice(h * D, (h + 1) * D)
        r = 1.0 / jnp.maximum(jnp.abs(den[h]), floor_rep[h])
        f = r * lax.rsqrt(r * r * msq[h] + EPS)
        y_ref[:, hs] = (jax.nn.sigmoid(o_ref[:, hs].astype(F32)) * (num[h] * f)
                        * ng_ref[:, hs]).astype(BF16)


def _mlstm(z, k_t, gates_c, gates_r, norm_g, batch, seq):
    L = MLSTM_CHUNK
    nt = seq // L
    tokens = batch * seq
    blk = M_WIDTH
    return pl.pallas_call(
        _mlstm_kernel,
        grid=(batch, nt),
        in_specs=[pl.BlockSpec((L, blk), lambda b, t: (b * nt + t, COL_MQ)),
                  pl.BlockSpec((blk, L), lambda b, t: (0, b * nt + t)),
                  pl.BlockSpec((L, blk), lambda b, t: (b * nt + t, COL_MV)),
                  pl.BlockSpec((L, blk), lambda b, t: (b * nt + t, COL_MO)),
                  pl.BlockSpec((L, LANES), lambda b, t: (b * nt + t, 0)),
                  pl.BlockSpec((2 * M_HEADS, L), lambda b, t: (0, b * nt + t)),
                  _const_spec((1, blk))],
        out_specs=pl.BlockSpec((L, blk), lambda b, t: (b * nt + t, 0)),
        out_shape=jax.ShapeDtypeStruct((tokens, blk), BF16),
        scratch_shapes=[pltpu.VMEM((M_HEADS, M_HEAD_DIM, 2 * M_HEAD_DIM), F32),
                        pltpu.VMEM((2 * M_HEADS, LANES), F32)],
        compiler_params=pltpu.CompilerParams(dimension_semantics=("parallel", "arbitrary"),
                                             vmem_limit_bytes=VMEM_LIMIT),
        name="mlstm",
    )(z, k_t, z, z, gates_c, gates_r, norm_g)


def _attn_kernel(qt_ref, k_ref, vt_ref, lam_ref, sg_ref, y_ref, m_ref, acc_ref, s0_ref, s1_ref,
                 qz_ref):
    TQ = ATTN_Q_BLOCK
    TK = ATTN_K_BLOCK
    D = A_HEAD_DIM
    i = pl.program_id(1)
    sub = lax.broadcasted_iota(jnp.int32, (D, 1), 0)
    for h in range(A_HEADS):
        qt = qt_ref[h * D:(h + 1) * D, :]
        zero = jnp.zeros_like(qt)
        qz_ref[h, :, 0:TQ] = jnp.where(sub < A_QK_DIM, qt, zero)
        qz_ref[h, :, TQ:2 * TQ] = jnp.where(sub >= A_QK_DIM, qt, zero)
    m_ref[...] = jnp.full_like(m_ref, NEG_BIG)
    acc_ref[...] = jnp.zeros_like(acc_ref)

    hi_cols = (slice(TK, TQ), slice(TQ + TK, 2 * TQ))

    def score_head(h, t, s_ref):
        start = pl.multiple_of(t * TK, TK)
        k = k_ref[pl.ds(start, TK), h * D:(h + 1) * D]
        s_ref[h] = jnp.dot(k, qz_ref[h], preferred_element_type=F32).astype(BF16)

    def update_head(h, t, s, cols=None):
        start = pl.multiple_of(t * TK, TK)
        m_old = m_ref[h:h + 1, :] if cols is None else \
            jnp.concatenate([m_ref[h:h + 1, c] for c in cols], axis=1)
        m_new = jnp.maximum(m_old, jnp.max(s, axis=0, keepdims=True).astype(F32))
        alpha = jnp.exp2(m_old - m_new)
        prob = jnp.exp2(s - m_new.astype(BF16))
        vt = vt_ref[h * VT_ROWS:(h + 1) * VT_ROWS, pl.ds(start, TK)]
        pv = jnp.dot(vt, prob, preferred_element_type=F32)
        if cols is None:
            m_ref[h:h + 1, :] = m_new
            acc_ref[h] = pv + alpha * acc_ref[h]
        else:
            for n, c in enumerate(cols):
                blk = slice(n * TK, (n + 1) * TK)
                m_ref[h:h + 1, c] = m_new[:, blk]
                acc_ref[h, :, c] = alpha[:, blk] * acc_ref[h, :, c] + pv[:, blk]

    for h in range(A_HEADS):
        score_head(h, 0, s0_ref)

    def body(jj, carry):
        for h in range(A_HEADS):
            score_head(h, 2 * jj + 1, s1_ref)
            update_head(h, 2 * jj, s0_ref[h])
        for h in range(A_HEADS):
            score_head(h, 2 * jj + 2, s0_ref)
            update_head(h, 2 * jj + 1, s1_ref[h])
        return carry

    lax.fori_loop(0, i, body, 0)

    visible = lax.broadcasted_iota(jnp.int32, (TK, TK), 0) <= \
        lax.broadcasted_iota(jnp.int32, (TK, TK), 1)
    start_b = pl.multiple_of((2 * i + 1) * TK, TK)
    for h in range(A_HEADS):
        k = k_ref[pl.ds(start_b, TK), h * D:(h + 1) * D]
        for n, c in enumerate(hi_cols):
            s1_ref[h, :, n * TK:(n + 1) * TK] = jnp.dot(
                k, qz_ref[h, :, c], preferred_element_type=F32).astype(BF16)

    def diag_first(h):
        s = s0_ref[h]
        parts = [s[:, n * TK:(n + 1) * TK] for n in range(2 * TQ // TK)]
        for n in (0, TQ // TK):
            parts[n] = jnp.where(visible, parts[n], NEG_BIG)
        return jnp.concatenate(parts, axis=1)

    def diag_second(h):
        return jnp.concatenate([jnp.where(visible, s1_ref[h, :, n * TK:(n + 1) * TK], NEG_BIG)
                                for n in range(len(hi_cols))], axis=1)

    for h in range(A_HEADS):
        update_head(h, 2 * i, diag_first(h))
    for h in range(A_HEADS):
        update_head(h, 2 * i + 1, diag_second(h), hi_cols)

    lq1, lk1, lq2, lk2 = lam_ref[0:1, :], lam_ref[1:2, :], lam_ref[2:3, :], lam_ref[3:4, :]
    lam = jnp.exp(jnp.sum(lq1 * lk1, axis=-1, keepdims=True)) \
        - jnp.exp(jnp.sum(lq2 * lk2, axis=-1, keepdims=True)) + LAMBDA_INIT
    for h in range(A_HEADS):
        hs = slice(h * D, (h + 1) * D)
        acc = acc_ref[h]
        ot = acc[:D, :] / acc[D:D + 1, :]
        o = (ot[:, :TQ] - lam * ot[:, TQ:]).T
        on = o * lax.rsqrt(jnp.mean(o * o, axis=-1, keepdims=True) + EPS) * sg_ref[:, hs]
        y_ref[:, hs] = (on * (1.0 - LAMBDA_INIT)).astype(BF16)


def _attention(z, q_t, v_t, lam_vecs, sub_g, batch, seq):
    TQ = ATTN_Q_BLOCK
    nq = seq // TQ
    tokens = batch * seq
    score_buf = pltpu.VMEM((A_HEADS, ATTN_K_BLOCK, 2 * TQ), BF16)
    return pl.pallas_call(
        _attn_kernel,
        grid=(batch, nq),
        in_specs=[pl.BlockSpec((A_WIDTH, TQ), lambda b, i: (0, b * nq + i)),
                  pl.BlockSpec((seq, A_WIDTH), lambda b, i: (b, COL_AK)),
                  pl.BlockSpec((A_HEADS * VT_ROWS, seq), lambda b, i: (0, b)),
                  _const_spec((4, LANES)), _const_spec((1, A_WIDTH))],
        out_specs=pl.BlockSpec((TQ, A_WIDTH), lambda b, i: (b * nq + i, 0)),
        out_shape=jax.ShapeDtypeStruct((tokens, A_WIDTH), BF16),
        scratch_shapes=[pltpu.VMEM((2 * A_HEADS, 2 * TQ), F32),
                        pltpu.VMEM((A_HEADS, VT_ROWS, 2 * TQ), F32),
                        score_buf, score_buf,
                        pltpu.VMEM((A_HEADS, A_HEAD_DIM, 2 * TQ), BF16)],
        compiler_params=pltpu.CompilerParams(dimension_semantics=("parallel", "arbitrary"),
                                             vmem_limit_bytes=VMEM_LIMIT),
        name="diff_attn",
    )(q_t, z, v_t, lam_vecs, sub_g)


def _rms(x, g):
    return x * lax.rsqrt(jnp.mean(x * x, axis=-1, keepdims=True) + EPS) * g


def _ffn_kernel(x_ref, ym_ref, ya_ref, p_ref, wo_ref, g2_ref, wu_ref, wd_ref, g3_ref, wg_ref,
                wp_ref, out_ref):
    h = x_ref[...] \
        + jnp.dot(ym_ref[...], wo_ref[0:M_WIDTH, :], preferred_element_type=F32) \
        + jnp.dot(ya_ref[...], wo_ref[M_WIDTH:, :], preferred_element_type=F32)
    u2 = _rms(h, g2_ref[...]).astype(BF16)
    mlp = None
    for c in range(wu_ref.shape[1] // FF_CHUNK):
        cols = slice(c * FF_CHUNK, (c + 1) * FF_CHUNK)
        hid = jnp.maximum(jnp.dot(u2, wu_ref[:, cols], preferred_element_type=F32), 0.0)
        part = jnp.dot((hid * hid).astype(BF16), wd_ref[cols, :], preferred_element_type=F32)
        mlp = part if mlp is None else mlp + part
    h = h + mlp
    u3 = _rms(h, g3_ref[...]).astype(BF16)
    gate = jax.nn.sigmoid(jnp.dot(u3, wg_ref[...], preferred_element_type=F32))
    e = jnp.dot(p_ref[...].astype(BF16), wp_ref[...], preferred_element_type=F32)
    out_ref[...] = h + gate * e


def _ffn(x2, y_m, y_a, p2, w_out, g2, w_up, w_down, g3, w_gate, w_ple):
    tokens, d_model = x2.shape
    tm = FFN_ROWS
    row_spec = lambda w: pl.BlockSpec((tm, w), lambda i: (i, 0))
    return pl.pallas_call(
        _ffn_kernel,
        grid=(tokens // tm,),
        in_specs=[row_spec(d_model), row_spec(M_WIDTH), row_spec(A_WIDTH), row_spec(p2.shape[1]),
                  _const_spec(w_out.shape), _const_spec((1, d_model)), _const_spec(w_up.shape),
                  _const_spec(w_down.shape), _const_spec((1, d_model)),
                  _const_spec(w_gate.shape), _const_spec(w_ple.shape)],
        out_specs=row_spec(d_model),
        out_shape=jax.ShapeDtypeStruct((tokens, d_model), F32),
        compiler_params=pltpu.CompilerParams(dimension_semantics=("parallel",),
                                             vmem_limit_bytes=VMEM_LIMIT),
        name="ffn",
    )(x2, y_m, y_a, p2, w_out, g2, w_up, w_down, g3, w_gate, w_ple)


def _rope_tables(seq):
    pos = jnp.arange(seq, dtype=F32)
    inv_freq = ROPE_THETA ** (-jnp.arange(0, ROPE_DIM, 2, dtype=F32) / ROPE_DIM)
    ang = pos[:, None] * inv_freq[None, :]
    cos, sin = jnp.cos(ang), jnp.sin(ang)
    pad = jnp.zeros((seq, A_QK_DIM - ROPE_DIM), F32)
    zero8 = jnp.zeros((seq, ROPE_HALF), F32)
    ct = jnp.concatenate([cos, cos, pad + 1.0], axis=1)
    su = jnp.concatenate([zero8, sin, pad], axis=1)
    sd = jnp.concatenate([-sin, zero8, pad], axis=1)
    rep = LANES // A_QK_DIM
    return tuple(jnp.tile(t, (1, rep)) for t in (ct, su, sd))


def kernel(x, p, attn_norm_g, w_in, conv_w, conv_b, igate_b, fgate_b, mlstm_norm_g, q_norm_g,
           k_norm_g, lambda_q1, lambda_k1, lambda_q2, lambda_k2, attn_sub_norm_g, w_out,
           mlp_norm_g, w_up, w_down, ple_norm_g, w_ple_gate, w_ple_proj):
    batch, seq, d_model = x.shape
    assert w_in.shape[0] == 1, "single-layer block"
    assert seq % PROJ_ROWS == 0 and seq % MLSTM_CHUNK == 0 and seq % ATTN_Q_BLOCK == 0
    tokens = batch * seq
    x2 = x.reshape(tokens, d_model).astype(F32)
    p2 = p.reshape(tokens, p.shape[-1])

    w = w_in[0]
    gate_lo = 4 * M_WIDTH
    gate_hi = gate_lo + 2 * M_HEADS
    w_main = jnp.concatenate([w[:, :gate_lo], w[:, gate_hi:]], axis=1).astype(BF16)
    w_gatecols = jnp.pad(w[:, gate_lo:gate_hi], ((0, 0), (0, LANES - 2 * M_HEADS))).astype(BF16)
    gate_b = jnp.pad(jnp.concatenate([igate_b[0], fgate_b[0]]),
                     (0, LANES - 2 * M_HEADS)).reshape(1, LANES).astype(F32)
    rep = LANES // A_QK_DIM
    q_gain = jnp.tile(q_norm_g[0].astype(F32), rep).reshape(1, LANES)
    k_gain = jnp.tile(k_norm_g[0].astype(F32), rep).reshape(1, LANES)
    ct, su, sd = _rope_tables(seq)

    z, gates_c, gates_r, q_t, v_t, k_t = _project(
        x2, attn_norm_g[0].reshape(1, d_model).astype(F32), w_main, w_gatecols, gate_b, q_gain,
        k_gain, ct, su, sd, conv_w[0].astype(F32), conv_b[0].reshape(1, -1).astype(F32), seq)

    y_m = _mlstm(z, k_t, gates_c, gates_r, mlstm_norm_g[0].reshape(1, -1).astype(F32), batch, seq)

    lam_vecs = jnp.pad(jnp.stack([lambda_q1[0], lambda_k1[0], lambda_q2[0], lambda_k2[0]]),
                       ((0, 0), (0, LANES - A_QK_DIM))).astype(F32)
    y_a = _attention(z, q_t, v_t, lam_vecs, attn_sub_norm_g[0].reshape(1, -1).astype(F32), batch, seq)

    out = _ffn(x2, y_m, y_a, p2, w_out[0].astype(BF16),
               mlp_norm_g[0].reshape(1, d_model).astype(F32), w_up[0].astype(BF16),
               w_down[0].astype(BF16), ple_norm_g[0].reshape(1, d_model).astype(F32),
               w_ple_gate[0].astype(BF16), w_ple_proj[0].astype(BF16))
    return out.reshape(batch, seq, d_model).astype(x.dtype)
```

```python
import functools
import math

import jax
import jax.numpy as jnp
import numpy as np
from jax import lax
from jax.experimental import pallas as pl
from jax.experimental.pallas import tpu as pltpu

F32 = jnp.float32
BF16 = jnp.bfloat16

EPS = 1e-6
LANES = 128
M_HEADS = 4
M_HEAD_DIM = 128
M_WIDTH = M_HEADS * M_HEAD_DIM
CONV_WIDTH = 4
A_HEADS = 4
A_HEAD_DIM = 128
A_QK_DIM = 64
A_WIDTH = A_HEADS * A_HEAD_DIM
ROPE_THETA = 500000.0
ROPE_DIM = A_QK_DIM // 4
ROPE_HALF = ROPE_DIM // 2
LAMBDA_INIT = 0.8 - 0.6 * math.exp(-0.3 * 0)
LOG2E = math.log2(math.e)

COL_MQ, COL_MK, COL_MV, COL_MO, COL_AQ, COL_AK, COL_AV = range(7)
N_GROUPS = 7
Z_WIDTH = N_GROUPS * M_WIDTH
ATT_GROUPS = 3
GATE_COL0 = 4 * M_WIDTH
ATT_COL0 = GATE_COL0 + 2 * M_HEADS

PROJ_ROWS = 1024
MLSTM_CHUNK = 256
ATTN_K_BLOCK = 256
ATTN_Q_BLOCK = 2 * ATTN_K_BLOCK
FFN_ROWS = 1024
FF_CHUNK = 1024
ONES_ROWS = 16
VT_ROWS = A_HEAD_DIM + ONES_ROWS
CARRY_ROWS = 8
NEG_BIG = -2.0 ** 100
VMEM_LIMIT = 56 * 1024 * 1024
PROJ_VMEM_LIMIT = 60000 * 1024


def _const_spec(shape):
    return pl.BlockSpec(shape, lambda *_: (0,) * len(shape), pipeline_mode=pl.Buffered(1))


def _group_mean_square(sq, lo_mask):
    tot = jnp.sum(sq, axis=-1, keepdims=True)
    lo = jnp.sum(jnp.where(lo_mask, sq, 0.0), axis=-1, keepdims=True)
    return jnp.where(lo_mask, lo, tot - lo) * (1.0 / A_QK_DIM)


def _qk_norm_rope(z, gain, ct, su, sd, scale):
    lane = lax.broadcasted_iota(jnp.int32, (1, LANES), 1)
    lo_mask = lane < A_QK_DIM
    outs = []
    for j in range(z.shape[1] // LANES):
        blk = z[:, j * LANES:(j + 1) * LANES]
        ms = _group_mean_square(blk * blk, lo_mask)
        xg = blk * lax.rsqrt(ms + EPS) * gain
        up = pltpu.roll(xg, ROPE_HALF, axis=1)
        dn = pltpu.roll(xg, LANES - ROPE_HALF, axis=1)
        outs.append((xg * ct + up * su + dn * sd) * scale)
    return jnp.concatenate(outs, axis=1)


def _log_sigmoid(x):
    return jnp.minimum(x, 0.0) - jnp.log1p(jnp.exp(-jnp.abs(x)))


def _conv_silu(xbuf_ref, cols, rows, cw, cb):
    conv = cb + cw[CONV_WIDTH - 1:CONV_WIDTH, :] * xbuf_ref[CARRY_ROWS:, cols]
    for back in range(1, CONV_WIDTH):
        tap = CONV_WIDTH - 1 - back
        conv = conv + cw[tap:tap + 1, :] * xbuf_ref[pl.ds(CARRY_ROWS - back, rows), cols]
    xbuf_ref[0:CARRY_ROWS, cols] = xbuf_ref[rows:rows + CARRY_ROWS, cols]
    return conv * jax.nn.sigmoid(conv)


PROJ_ORDER = (COL_AQ, COL_AK, COL_AV, COL_MQ, COL_MK, COL_MV, COL_MO)
STAGED = {COL_MQ: None, COL_MK: None, COL_AQ: 0, COL_AK: 1, COL_AV: 0}


def _proj_kernel(x_ref, g_ref, w_ref, gb_ref, qg_ref, kg_ref, ct_ref, su_ref, sd_ref,
                 cw_ref, cb_ref, z_ref, gc_ref, gr_ref, qt_ref, vt_ref, kt_ref, xbuf_ref, zbuf_ref,
                 watt_ref, *, pos_blocks):
    rows = x_ref.shape[0]

    @pl.when(pl.program_id(0) % pos_blocks == 0)
    def _():
        xbuf_ref[0:CARRY_ROWS, :] = jnp.zeros((CARRY_ROWS, 2 * M_WIDTH), F32)

    @pl.when(pl.program_id(0) == 0)
    def _():
        for j in range(ATT_GROUPS):
            watt_ref[:, j * M_WIDTH:(j + 1) * M_WIDTH] = \
                w_ref[:, ATT_COL0 + j * M_WIDTH:ATT_COL0 + (j + 1) * M_WIDTH]

    x = x_ref[...]
    ms = jnp.mean(x * x, axis=-1, keepdims=True)
    u = (x * lax.rsqrt(ms + EPS) * g_ref[...]).astype(BF16)

    def stage(c):
        cols = slice(c * M_WIDTH, (c + 1) * M_WIDTH)
        w = w_ref[:, cols] if c < COL_AQ else \
            watt_ref[:, (c - COL_AQ) * M_WIDTH:(c - COL_AQ + 1) * M_WIDTH]
        zc = jnp.dot(u, w, preferred_element_type=F32)
        if c in (COL_MQ, COL_MK):
            xbuf_ref[CARRY_ROWS:, cols] = zc
        elif c in STAGED:
            zbuf_ref[STAGED[c]] = zc
        else:
            z_ref[:, cols] = zc.astype(BF16)

    def epilogue(c):
        cols = slice(c * M_WIDTH, (c + 1) * M_WIDTH)
        if c == COL_MQ:
            zc = _conv_silu(xbuf_ref, cols, rows, cw_ref[:, cols], cb_ref[:, cols]) \
                * M_HEAD_DIM ** -0.5
        elif c == COL_MK:
            zc = _conv_silu(xbuf_ref, cols, rows, cw_ref[:, cols], cb_ref[:, cols])
        elif c == COL_AQ:
            zc = _qk_norm_rope(zbuf_ref[STAGED[c]], qg_ref[...], ct_ref[...], su_ref[...],
                               sd_ref[...], A_QK_DIM ** -0.5 * LOG2E)
        elif c == COL_AK:
            zc = _qk_norm_rope(zbuf_ref[STAGED[c]], kg_ref[...], ct_ref[...], su_ref[...],
                               sd_ref[...], 1.0)
        elif c == COL_AV:
            zc = zbuf_ref[STAGED[c]]
        else:
            return
        z_ref[:, cols] = zc.astype(BF16)
        if c == COL_MK:
            kt_ref[...] = zc.T.astype(BF16)
        elif c == COL_AQ:
            qt_ref[...] = zc.T.astype(BF16)
        elif c == COL_AV:
            zt = zc.T.astype(BF16)
            for h in range(A_HEADS):
                vt_ref[h * VT_ROWS:h * VT_ROWS + A_HEAD_DIM, :] = \
                    zt[h * A_HEAD_DIM:(h + 1) * A_HEAD_DIM, :]
                vt_ref[h * VT_ROWS + A_HEAD_DIM:(h + 1) * VT_ROWS, :] = \
                    jnp.ones((ONES_ROWS, rows), BF16)

    stage(PROJ_ORDER[0])
    for prev, cur in zip(PROJ_ORDER[:-1], PROJ_ORDER[1:]):
        stage(cur)
        epilogue(prev)
    gz = jnp.dot(u, w_ref[:, GATE_COL0:GATE_COL0 + LANES], preferred_element_type=F32) + gb_ref[...]
    epilogue(PROJ_ORDER[-1])
    lane = lax.broadcasted_iota(jnp.int32, (1, LANES), 1)
    gz = jnp.where(lane >= M_HEADS, _log_sigmoid(gz), gz)
    gc_ref[...] = gz
    gr_ref[...] = gz.T[:2 * M_HEADS, :]


def _project(x2, attn_norm_g, w_bf, gate_b, q_gain, k_gain, ct, su, sd, conv_w, conv_b, seq):
    tokens = x2.shape[0]
    d_model = x2.shape[1]
    tm = PROJ_ROWS
    pos_blocks = seq // tm
    row_spec = lambda w: pl.BlockSpec((tm, w), lambda i: (i, 0))
    tab_spec = pl.BlockSpec((tm, LANES), lambda i: (i % pos_blocks, 0))
    return pl.pallas_call(
        functools.partial(_proj_kernel, pos_blocks=pos_blocks),
        grid=(tokens // tm,),
        in_specs=[row_spec(d_model), _const_spec((1, d_model)), _const_spec(w_bf.shape),
                  _const_spec((1, LANES)), _const_spec((1, LANES)),
                  _const_spec((1, LANES)), tab_spec, tab_spec, tab_spec,
                  _const_spec((CONV_WIDTH, 2 * M_WIDTH)), _const_spec((1, 2 * M_WIDTH))],
        out_specs=[row_spec(Z_WIDTH), row_spec(LANES),
                   pl.BlockSpec((2 * M_HEADS, tm), lambda i: (0, i)),
                   pl.BlockSpec((A_WIDTH, tm), lambda i: (0, i)),
                   pl.BlockSpec((A_HEADS * VT_ROWS, tm), lambda i: (0, i)),
                   pl.BlockSpec((M_WIDTH, tm), lambda i: (0, i))],
        out_shape=[jax.ShapeDtypeStruct((tokens, Z_WIDTH), BF16),
                   jax.ShapeDtypeStruct((tokens, LANES), F32),
                   jax.ShapeDtypeStruct((2 * M_HEADS, tokens), F32),
                   jax.ShapeDtypeStruct((A_WIDTH, tokens), BF16),
                   jax.ShapeDtypeStruct((A_HEADS * VT_ROWS, tokens), BF16),
                   jax.ShapeDtypeStruct((M_WIDTH, tokens), BF16)],
        scratch_shapes=[pltpu.VMEM((tm + CARRY_ROWS, 2 * M_WIDTH), F32),
                        pltpu.VMEM((2, tm, M_WIDTH), F32),
                        pltpu.VMEM((d_model, ATT_GROUPS * M_WIDTH), BF16)],
        compiler_params=pltpu.CompilerParams(dimension_semantics=("arbitrary",),
                                             vmem_limit_bytes=PROJ_VMEM_LIMIT),
        name="proj",
    )(x2, attn_norm_g, w_bf, gate_b, q_gain, k_gain, ct, su, sd, conv_w, conv_b)


def _mlstm_kernel(q_ref, kt_ref, v_ref, o_ref, gc_ref, gr_ref, ng_ref, y_ref, state_ref, m_ref):
    L = MLSTM_CHUNK
    D = M_HEAD_DIM
    H = M_HEADS

    @pl.when(pl.program_id(1) == 0)
    def _():
        state_ref[...] = jnp.zeros_like(state_ref)
        m_ref[...] = jnp.zeros_like(m_ref)

    gc = gc_ref[...]
    gr = gr_ref[...]
    row = lax.broadcasted_iota(jnp.int32, (L, L), 0)
    col = lax.broadcasted_iota(jnp.int32, (L, L), 1)
    causal = col <= row
    lower = causal.astype(F32)
    upper = (row <= col).astype(F32)
    bc_all = jnp.dot(lower, gc, preferred_element_type=F32, precision=lax.Precision.HIGHEST)
    br_all = jnp.dot(gr, upper, preferred_element_type=F32, precision=lax.Precision.HIGHEST)
    ones_blk = jnp.ones((L, D), BF16)
    mean_mat = jnp.full((D, D), 1.0 / D, BF16)

    q = [q_ref[:, h * D:(h + 1) * D] for h in range(H)]
    v_aug = [jnp.concatenate([v_ref[:, h * D:(h + 1) * D], ones_blk], axis=1) for h in range(H)]
    m_prev = [m_ref[h:h + 1, 0:1] for h in range(H)]
    b_c = [bc_all[:, H + h:H + h + 1] for h in range(H)]
    b_r = [br_all[H + h:H + h + 1, :] for h in range(H)]
    li_r = [gr[h:h + 1, :] for h in range(H)]
    c_state = [state_ref[h] for h in range(H)]

    s = [jnp.dot(q[h], kt_ref[h * D:(h + 1) * D, :], preferred_element_type=F32)
         for h in range(H)]
    inter = [jnp.dot(q[h], c_state[h].astype(BF16), preferred_element_type=F32)
             for h in range(H)]

    b_last = [b_r[h][:, L - 1:L] for h in range(H)]
    g_r = [b_last[h] - b_r[h] + li_r[h] for h in range(H)]
    m_new = [jnp.maximum(b_last[h] + m_prev[h], jnp.max(g_r[h], axis=-1, keepdims=True))
             for h in range(H)]
    kt_scaled = [(kt_ref[h * D:(h + 1) * D, :].astype(F32)
                  * jnp.exp(g_r[h] - m_new[h])).astype(BF16) for h in range(H)]
    for h in range(H):
        decay = jnp.exp(b_last[h] + m_prev[h] - m_new[h])
        state_ref[h] = decay * c_state[h] + jnp.dot(kt_scaled[h], v_aug[h],
                                                    preferred_element_type=F32)
        m_ref[h:h + 1, :] = jnp.broadcast_to(m_new[h], (1, LANES))

    e = [jnp.where(causal, li_r[h] - b_r[h], NEG_BIG) for h in range(H)]
    u = [jnp.maximum(m_prev[h], jnp.max(e[h], axis=-1, keepdims=True)) for h in range(H)]
    u_rep = [jnp.broadcast_to(u[h], (L, D)) for h in range(H)]
    smat = [(s[h] * jnp.exp(e[h] - jnp.concatenate([u_rep[h]] * (L // D), axis=1))).astype(BF16)
            for h in range(H)]
    intra = [jnp.dot(smat[h], v_aug[h], preferred_element_type=F32) for h in range(H)]

    w_rep = [jnp.exp(m_prev[h] - u_rep[h]) for h in range(H)]
    num = [w_rep[h] * inter[h][:, :D] + intra[h][:, :D] for h in range(H)]
    den = [w_rep[h] * inter[h][:, D:] + intra[h][:, D:] for h in range(H)]
    floor_rep = [jnp.broadcast_to(jnp.exp(-(b_c[h] + u[h])), (L, D)) for h in range(H)]
    msq = [jnp.dot((num[h] * num[h]).astype(BF16), mean_mat, preferred_element_type=F32)
           for h in range(H)]
    for h in range(H):
        hs = slice(h * D, (h + 1) * D)
        r = 1.0 / jnp.maximum(jnp.abs(den[h]), floor_rep[h])
        f = r * lax.rsqrt(r * r * msq[h] + EPS)
        y_ref[:, hs] = (jax.nn.sigmoid(o_ref[:, hs].astype(F32)) * (num[h] * f)
                        * ng_ref[:, hs]).astype(BF16)


def _mlstm(z, k_t, gates_c, gates_r, norm_g, batch, seq):
    L = MLSTM_CHUNK
    nt = seq // L
    tokens = batch * seq
    blk = M_WIDTH
    return pl.pallas_call(
        _mlstm_kernel,
        grid=(batch, nt),
        in_specs=[pl.BlockSpec((L, blk), lambda b, t: (b * nt + t, COL_MQ)),
                  pl.BlockSpec((blk, L), lambda b, t: (0, b * nt + t)),
                  pl.BlockSpec((L, blk), lambda b, t: (b * nt + t, COL_MV)),
                  pl.BlockSpec((L, blk), lambda b, t: (b * nt + t, COL_MO)),
                  pl.BlockSpec((L, LANES), lambda b, t: (b * nt + t, 0)),
                  pl.BlockSpec((2 * M_HEADS, L), lambda b, t: (0, b * nt + t)),
                  _const_spec((1, blk))],
        out_specs=pl.BlockSpec((L, blk), lambda b, t: (b * nt + t, 0)),
        out_shape=jax.ShapeDtypeStruct((tokens, blk), BF16),
        scratch_shapes=[pltpu.VMEM((M_HEADS, M_HEAD_DIM, 2 * M_HEAD_DIM), F32),
                        pltpu.VMEM((2 * M_HEADS, LANES), F32)],
        compiler_params=pltpu.CompilerParams(dimension_semantics=("parallel", "arbitrary"),
                                             vmem_limit_bytes=VMEM_LIMIT),
        name="mlstm",
    )(z, k_t, z, z, gates_c, gates_r, norm_g)


def _attn_kernel(qt_ref, k_ref, vt_ref, lam_ref, sg_ref, y_ref, m_ref, acc_ref, s0_ref, s1_ref,
                 qz_ref):
    TQ = ATTN_Q_BLOCK
    TK = ATTN_K_BLOCK
    D = A_HEAD_DIM
    i = pl.program_id(1)
    sub = lax.broadcasted_iota(jnp.int32, (D, 1), 0)
    for h in range(A_HEADS):
        qt = qt_ref[h * D:(h + 1) * D, :]
        zero = jnp.zeros_like(qt)
        qz_ref[h, :, 0:TQ] = jnp.where(sub < A_QK_DIM, qt, zero)
        qz_ref[h, :, TQ:2 * TQ] = jnp.where(sub >= A_QK_DIM, qt, zero)
    m_ref[...] = jnp.full_like(m_ref, NEG_BIG)
    acc_ref[...] = jnp.zeros_like(acc_ref)

    hi_cols = (slice(TK, TQ), slice(TQ + TK, 2 * TQ))

    def score_head(h, t, s_ref):
        start = pl.multiple_of(t * TK, TK)
        k = k_ref[pl.ds(start, TK), h * D:(h + 1) * D]
        s_ref[h] = jnp.dot(k, qz_ref[h], preferred_element_type=F32).astype(BF16)

    def update_head(h, t, s, cols=None):
        start = pl.multiple_of(t * TK, TK)
        m_old = m_ref[h:h + 1, :] if cols is None else \
            jnp.concatenate([m_ref[h:h + 1, c] for c in cols], axis=1)
        m_new = jnp.maximum(m_old, jnp.max(s, axis=0, keepdims=True).astype(F32))
        alpha = jnp.exp2(m_old - m_new)
        prob = jnp.exp2(s - m_new.astype(BF16))
        vt = vt_ref[h * VT_ROWS:(h + 1) * VT_ROWS, pl.ds(start, TK)]
        pv = jnp.dot(vt, prob, preferred_element_type=F32)
        if cols is None:
            m_ref[h:h + 1, :] = m_new
            acc_ref[h] = pv + alpha * acc_ref[h]
        else:
            for n, c in enumerate(cols):
                blk = slice(n * TK, (n + 1) * TK)
                m_ref[h:h + 1, c] = m_new[:, blk]
                acc_ref[h, :, c] = alpha[:, blk] * acc_ref[h, :, c] + pv[:, blk]

    for h in range(A_HEADS):
        score_head(h, 0, s0_ref)

    def body(jj, carry):
        for h in range(A_HEADS):
            score_head(h, 2 * jj + 1, s1_ref)
            update_head(h, 2 * jj, s0_ref[h])
        for h in range(A_HEADS):
            score_head(h, 2 * jj + 2, s0_ref)
            update_head(h, 2 * jj + 1, s1_ref[h])
        return carry

    lax.fori_loop(0, i, body, 0)

    visible = lax.broadcasted_iota(jnp.int32, (TK, TK), 0) <= \
        lax.broadcasted_iota(jnp.int32, (TK, TK), 1)
    start_b = pl.multiple_of((2 * i + 1) * TK, TK)
    for h in range(A_HEADS):
        k = k_ref[pl.ds(start_b, TK), h * D:(h + 1) * D]
        for n, c in enumerate(hi_cols):
            s1_ref[h, :, n * TK:(n + 1) * TK] = jnp.dot(
                k, qz_ref[h, :, c], preferred_element_type=F32).astype(BF16)

    def diag_first(h):
        s = s0_ref[h]
        parts = [s[:, n * TK:(n + 1) * TK] for n in range(2 * TQ // TK)]
        for n in (0, TQ // TK):
            parts[n] = jnp.where(visible, parts[n], NEG_BIG)
        return jnp.concatenate(parts, axis=1)

    def diag_second(h):
        return jnp.concatenate([jnp.where(visible, s1_ref[h, :, n * TK:(n + 1) * TK], NEG_BIG)
                                for n in range(len(hi_cols))], axis=1)

    for h in range(A_HEADS):
        update_head(h, 2 * i, diag_first(h))
    for h in range(A_HEADS):
        update_head(h, 2 * i + 1, diag_second(h), hi_cols)

    lq1, lk1, lq2, lk2 = lam_ref[0:1, :], lam_ref[1:2, :], lam_ref[2:3, :], lam_ref[3:4, :]
    lam = jnp.exp(jnp.sum(lq1 * lk1, axis=-1, keepdims=True)) \
        - jnp.exp(jnp.sum(lq2 * lk2, axis=-1, keepdims=True)) + LAMBDA_INIT
    for h in range(A_HEADS):
        hs = slice(h * D, (h + 1) * D)
        acc = acc_ref[h]
        ot = acc[:D, :] / acc[D:D + 1, :]
        o = (ot[:, :TQ] - lam * ot[:, TQ:]).T
        on = o * lax.rsqrt(jnp.mean(o * o, axis=-1, keepdims=True) + EPS) * sg_ref[:, hs]
        y_ref[:, hs] = (on * (1.0 - LAMBDA_INIT)).astype(BF16)


def _attention(z, q_t, v_t, lam_vecs, sub_g, batch, seq):
    TQ = ATTN_Q_BLOCK
    nq = seq // TQ
    tokens = batch * seq
    score_buf = pltpu.VMEM((A_HEADS, ATTN_K_BLOCK, 2 * TQ), BF16)
    return pl.pallas_call(
        _attn_kernel,
        grid=(batch, nq),
        in_specs=[pl.BlockSpec((A_WIDTH, TQ), lambda b, i: (0, b * nq + i)),
                  pl.BlockSpec((seq, A_WIDTH), lambda b, i: (b, COL_AK)),
                  pl.BlockSpec((A_HEADS * VT_ROWS, seq), lambda b, i: (0, b)),
                  _const_spec((4, LANES)), _const_spec((1, A_WIDTH))],
        out_specs=pl.BlockSpec((TQ, A_WIDTH), lambda b, i: (b * nq + i, 0)),
        out_shape=jax.ShapeDtypeStruct((tokens, A_WIDTH), BF16),
        scratch_shapes=[pltpu.VMEM((2 * A_HEADS, 2 * TQ), F32),
                        pltpu.VMEM((A_HEADS, VT_ROWS, 2 * TQ), F32),
                        score_buf, score_buf,
                        pltpu.VMEM((A_HEADS, A_HEAD_DIM, 2 * TQ), BF16)],
        compiler_params=pltpu.CompilerParams(dimension_semantics=("parallel", "arbitrary"),
                                             vmem_limit_bytes=VMEM_LIMIT),
        name="diff_attn",
    )(q_t, z, v_t, lam_vecs, sub_g)


def _rms(x, g):
    return x * lax.rsqrt(jnp.mean(x * x, axis=-1, keepdims=True) + EPS) * g


def _ffn_kernel(x_ref, ym_ref, ya_ref, p_ref, wo_ref, g2_ref, wu_ref, wd_ref, g3_ref, wg_ref,
                wp_ref, out_ref):
    h = x_ref[...] \
        + jnp.dot(ym_ref[...], wo_ref[0:M_WIDTH, :], preferred_element_type=F32) \
        + jnp.dot(ya_ref[...], wo_ref[M_WIDTH:, :], preferred_element_type=F32)
    u2 = _rms(h, g2_ref[...]).astype(BF16)
    mlp = None
    for c in range(wu_ref.shape[1] // FF_CHUNK):
        cols = slice(c * FF_CHUNK, (c + 1) * FF_CHUNK)
        hid = jnp.maximum(jnp.dot(u2, wu_ref[:, cols], preferred_element_type=F32), 0.0)
        part = jnp.dot((hid * hid).astype(BF16), wd_ref[cols, :], preferred_element_type=F32)
        mlp = part if mlp is None else mlp + part
    h = h + mlp
    u3 = _rms(h, g3_ref[...]).astype(BF16)
    gate = jax.nn.sigmoid(jnp.dot(u3, wg_ref[...], preferred_element_type=F32))
    e = jnp.dot(p_ref[...].astype(BF16), wp_ref[...], preferred_element_type=F32)
    out_ref[...] = h + gate * e


def _ffn(x2, y_m, y_a, p2, w_out, g2, w_up, w_down, g3, w_gate, w_ple):
    tokens, d_model = x2.shape
    tm = FFN_ROWS
    row_spec = lambda w: pl.BlockSpec((tm, w), lambda i: (i, 0))
    return pl.pallas_call(
        _ffn_kernel,
        grid=(tokens // tm,),
        in_specs=[row_spec(d_model), row_spec(M_WIDTH), row_spec(A_WIDTH), row_spec(p2.shape[1]),
                  _const_spec(w_out.shape), _const_spec((1, d_model)), _const_spec(w_up.shape),
                  _const_spec(w_down.shape), _const_spec((1, d_model)),
                  _const_spec(w_gate.shape), _const_spec(w_ple.shape)],
        out_specs=row_spec(d_model),
        out_shape=jax.ShapeDtypeStruct((tokens, d_model), F32),
        compiler_params=pltpu.CompilerParams(dimension_semantics=("parallel",),
                                             vmem_limit_bytes=VMEM_LIMIT),
        name="ffn",
    )(x2, y_m, y_a, p2, w_out, g2, w_up, w_down, g3, w_gate, w_ple)


def _rope_tables(seq):
    pos = np.arange(seq, dtype=np.float32)
    inv_freq = (ROPE_THETA ** (-np.arange(0, ROPE_DIM, 2, dtype=np.float32) / ROPE_DIM)
                ).astype(np.float32)
    ang = pos[:, None] * inv_freq[None, :]
    cos, sin = np.cos(ang).astype(np.float32), np.sin(ang).astype(np.float32)
    pad = np.zeros((seq, A_QK_DIM - ROPE_DIM), np.float32)
    zero8 = np.zeros((seq, ROPE_HALF), np.float32)
    ct = np.concatenate([cos, cos, pad + 1.0], axis=1)
    su = np.concatenate([zero8, sin, pad], axis=1)
    sd = np.concatenate([-sin, zero8, pad], axis=1)
    rep = LANES // A_QK_DIM
    return tuple(jnp.asarray(np.tile(t, (1, rep))) for t in (ct, su, sd))


def kernel(x, p, attn_norm_g, w_in, conv_w, conv_b, igate_b, fgate_b, mlstm_norm_g, q_norm_g,
           k_norm_g, lambda_q1, lambda_k1, lambda_q2, lambda_k2, attn_sub_norm_g, w_out,
           mlp_norm_g, w_up, w_down, ple_norm_g, w_ple_gate, w_ple_proj):
    batch, seq, d_model = x.shape
    assert w_in.shape[0] == 1, "single-layer block"
    assert seq % PROJ_ROWS == 0 and seq % MLSTM_CHUNK == 0 and seq % ATTN_Q_BLOCK == 0
    tokens = batch * seq
    x2 = x.reshape(tokens, d_model).astype(F32)
    p2 = p.reshape(tokens, p.shape[-1])

    w_bf = w_in[0].astype(BF16)
    gate_b = jnp.pad(jnp.concatenate([igate_b[0], fgate_b[0]]),
                     (0, LANES - 2 * M_HEADS)).reshape(1, LANES).astype(F32)
    rep = LANES // A_QK_DIM
    q_gain = jnp.tile(q_norm_g[0].astype(F32), rep).reshape(1, LANES)
    k_gain = jnp.tile(k_norm_g[0].astype(F32), rep).reshape(1, LANES)
    ct, su, sd = _rope_tables(seq)

    z, gates_c, gates_r, q_t, v_t, k_t = _project(
        x2, attn_norm_g[0].reshape(1, d_model).astype(F32), w_bf, gate_b, q_gain,
        k_gain, ct, su, sd, conv_w[0].astype(F32), conv_b[0].reshape(1, -1).astype(F32), seq)

    y_m = _mlstm(z, k_t, gates_c, gates_r, mlstm_norm_g[0].reshape(1, -1).astype(F32), batch, seq)

    lam_vecs = jnp.pad(jnp.stack([lambda_q1[0], lambda_k1[0], lambda_q2[0], lambda_k2[0]]),
                       ((0, 0), (0, LANES - A_QK_DIM))).astype(F32)
    y_a = _attention(z, q_t, v_t, lam_vecs, attn_sub_norm_g[0].reshape(1, -1).astype(F32), batch, seq)

    out = _ffn(x2, y_m, y_a, p2, w_out[0].astype(BF16),
               mlp_norm_g[0].reshape(1, d_model).astype(F32), w_up[0].astype(BF16),
               w_down[0].astype(BF16), ple_norm_g[0].reshape(1, d_model).astype(F32),
               w_ple_gate[0].astype(BF16), w_ple_proj[0].astype(BF16))
    return out.reshape(batch, seq, d_model).astype(x.dtype)
```

```python
import functools
import math

import jax
import jax.numpy as jnp
import numpy as np
from jax import lax
from jax.experimental import pallas as pl
from jax.experimental.pallas import tpu as pltpu

F32 = jnp.float32
BF16 = jnp.bfloat16

EPS = 1e-6
LANES = 128
M_HEADS = 4
M_HEAD_DIM = 128
M_WIDTH = M_HEADS * M_HEAD_DIM
CONV_WIDTH = 4
A_HEADS = 4
A_HEAD_DIM = 128
A_QK_DIM = 64
A_WIDTH = A_HEADS * A_HEAD_DIM
ROPE_THETA = 500000.0
ROPE_DIM = A_QK_DIM // 4
ROPE_HALF = ROPE_DIM // 2
LAMBDA_INIT = 0.8 - 0.6 * math.exp(-0.3 * 0)
LOG2E = math.log2(math.e)

COL_MQ, COL_MK, COL_MV, COL_MO, COL_AQ, COL_AK, COL_AV = range(7)
N_GROUPS = 7
Z_WIDTH = N_GROUPS * M_WIDTH
ATT_GROUPS = 3
GATE_COL0 = 4 * M_WIDTH
ATT_COL0 = GATE_COL0 + 2 * M_HEADS

PROJ_ROWS = 1024
MLSTM_CHUNK = 256
ATTN_K_BLOCK = 256
ATTN_Q_BLOCK = 2 * ATTN_K_BLOCK
FFN_ROWS = 512
FF_CHUNK = 1024
ONES_ROWS = 16
VT_ROWS = A_HEAD_DIM + ONES_ROWS
CARRY_ROWS = 8
NEG_BIG = -2.0 ** 100
VMEM_LIMIT = 56 * 1024 * 1024
PROJ_VMEM_LIMIT = 60000 * 1024


def _const_spec(shape):
    return pl.BlockSpec(shape, lambda *_: (0,) * len(shape), pipeline_mode=pl.Buffered(1))


def _group_mean_square(sq, lo_mask):
    tot = jnp.sum(sq, axis=-1, keepdims=True)
    lo = jnp.sum(jnp.where(lo_mask, sq, 0.0), axis=-1, keepdims=True)
    return jnp.where(lo_mask, lo, tot - lo) * (1.0 / A_QK_DIM)


def _qk_norm_rope(z, gain, ct, su, sd, scale):
    lane = lax.broadcasted_iota(jnp.int32, (1, LANES), 1)
    lo_mask = lane < A_QK_DIM
    outs = []
    for j in range(z.shape[1] // LANES):
        blk = z[:, j * LANES:(j + 1) * LANES]
        ms = _group_mean_square(blk * blk, lo_mask)
        xg = blk * lax.rsqrt(ms + EPS) * gain
        up = pltpu.roll(xg, ROPE_HALF, axis=1)
        dn = pltpu.roll(xg, LANES - ROPE_HALF, axis=1)
        outs.append((xg * ct + up * su + dn * sd) * scale)
    return jnp.concatenate(outs, axis=1)


def _log_sigmoid(x):
    return jnp.minimum(x, 0.0) - jnp.log1p(jnp.exp(-jnp.abs(x)))


def _conv_silu(xbuf_ref, cols, rows, cw, cb):
    conv = cb + cw[CONV_WIDTH - 1:CONV_WIDTH, :] * xbuf_ref[CARRY_ROWS:, cols]
    for back in range(1, CONV_WIDTH):
        tap = CONV_WIDTH - 1 - back
        conv = conv + cw[tap:tap + 1, :] * xbuf_ref[pl.ds(CARRY_ROWS - back, rows), cols]
    xbuf_ref[0:CARRY_ROWS, cols] = xbuf_ref[rows:rows + CARRY_ROWS, cols]
    return conv * jax.nn.sigmoid(conv)


PROJ_ORDER = (COL_AQ, COL_AK, COL_AV, COL_MQ, COL_MK, COL_MV, COL_MO)
STAGED = {COL_MQ: None, COL_MK: None, COL_AQ: 0, COL_AK: 1, COL_AV: 0}


def _proj_kernel(x_ref, g_ref, w_ref, gb_ref, qg_ref, kg_ref, ct_ref, su_ref, sd_ref,
                 cw_ref, cb_ref, z_ref, gc_ref, gr_ref, qt_ref, vt_ref, kt_ref, xbuf_ref, zbuf_ref,
                 watt_ref, *, pos_blocks):
    rows = x_ref.shape[0]

    @pl.when(pl.program_id(0) % pos_blocks == 0)
    def _():
        xbuf_ref[0:CARRY_ROWS, :] = jnp.zeros((CARRY_ROWS, 2 * M_WIDTH), F32)

    @pl.when(pl.program_id(0) == 0)
    def _():
        for j in range(ATT_GROUPS):
            watt_ref[:, j * M_WIDTH:(j + 1) * M_WIDTH] = \
                w_ref[:, ATT_COL0 + j * M_WIDTH:ATT_COL0 + (j + 1) * M_WIDTH]

    x = x_ref[...]
    ms = jnp.mean(x * x, axis=-1, keepdims=True)
    u = (x * lax.rsqrt(ms + EPS) * g_ref[...]).astype(BF16)

    def stage(c):
        cols = slice(c * M_WIDTH, (c + 1) * M_WIDTH)
        w = w_ref[:, cols] if c < COL_AQ else \
            watt_ref[:, (c - COL_AQ) * M_WIDTH:(c - COL_AQ + 1) * M_WIDTH]
        zc = jnp.dot(u, w, preferred_element_type=F32)
        if c in (COL_MQ, COL_MK):
            xbuf_ref[CARRY_ROWS:, cols] = zc
        elif c in STAGED:
            zbuf_ref[STAGED[c]] = zc
        else:
            z_ref[:, cols] = zc.astype(BF16)

    def epilogue(c):
        cols = slice(c * M_WIDTH, (c + 1) * M_WIDTH)
        if c == COL_MQ:
            zc = _conv_silu(xbuf_ref, cols, rows, cw_ref[:, cols], cb_ref[:, cols]) \
                * M_HEAD_DIM ** -0.5
        elif c == COL_MK:
            zc = _conv_silu(xbuf_ref, cols, rows, cw_ref[:, cols], cb_ref[:, cols])
        elif c == COL_AQ:
            zc = _qk_norm_rope(zbuf_ref[STAGED[c]], qg_ref[...], ct_ref[...], su_ref[...],
                               sd_ref[...], A_QK_DIM ** -0.5 * LOG2E)
        elif c == COL_AK:
            zc = _qk_norm_rope(zbuf_ref[STAGED[c]], kg_ref[...], ct_ref[...], su_ref[...],
                               sd_ref[...], 1.0)
        elif c == COL_AV:
            zc = zbuf_ref[STAGED[c]]
        else:
            return
        z_ref[:, cols] = zc.astype(BF16)
        if c == COL_MK:
            kt_ref[...] = zc.T.astype(BF16)
        elif c == COL_AQ:
            qt_ref[...] = zc.T.astype(BF16)
        elif c == COL_AV:
            zt = zc.T.astype(BF16)
            for h in range(A_HEADS):
                vt_ref[h * VT_ROWS:h * VT_ROWS + A_HEAD_DIM, :] = \
                    zt[h * A_HEAD_DIM:(h + 1) * A_HEAD_DIM, :]
                vt_ref[h * VT_ROWS + A_HEAD_DIM:(h + 1) * VT_ROWS, :] = \
                    jnp.ones((ONES_ROWS, rows), BF16)

    stage(PROJ_ORDER[0])
    for prev, cur in zip(PROJ_ORDER[:-1], PROJ_ORDER[1:]):
        stage(cur)
        epilogue(prev)
    gz = jnp.dot(u, w_ref[:, GATE_COL0:GATE_COL0 + LANES], preferred_element_type=F32) + gb_ref[...]
    epilogue(PROJ_ORDER[-1])
    lane = lax.broadcasted_iota(jnp.int32, (1, LANES), 1)
    gz = jnp.where(lane >= M_HEADS, _log_sigmoid(gz), gz)
    gc_ref[...] = gz
    gr_ref[...] = gz.T[:2 * M_HEADS, :]


def _project(x2, attn_norm_g, w_bf, gate_b, q_gain, k_gain, ct, su, sd, conv_w, conv_b, seq):
    tokens = x2.shape[0]
    d_model = x2.shape[1]
    tm = PROJ_ROWS
    pos_blocks = seq // tm
    row_spec = lambda w: pl.BlockSpec((tm, w), lambda i: (i, 0))
    tab_spec = pl.BlockSpec((tm, LANES), lambda i: (i % pos_blocks, 0))
    return pl.pallas_call(
        functools.partial(_proj_kernel, pos_blocks=pos_blocks),
        grid=(tokens // tm,),
        in_specs=[row_spec(d_model), _const_spec((1, d_model)), _const_spec(w_bf.shape),
                  _const_spec((1, LANES)), _const_spec((1, LANES)),
                  _const_spec((1, LANES)), tab_spec, tab_spec, tab_spec,
                  _const_spec((CONV_WIDTH, 2 * M_WIDTH)), _const_spec((1, 2 * M_WIDTH))],
        out_specs=[row_spec(Z_WIDTH), row_spec(LANES),
                   pl.BlockSpec((2 * M_HEADS, tm), lambda i: (0, i)),
                   pl.BlockSpec((A_WIDTH, tm), lambda i: (0, i)),
                   pl.BlockSpec((A_HEADS * VT_ROWS, tm), lambda i: (0, i)),
                   pl.BlockSpec((M_WIDTH, tm), lambda i: (0, i))],
        out_shape=[jax.ShapeDtypeStruct((tokens, Z_WIDTH), BF16),
                   jax.ShapeDtypeStruct((tokens, LANES), F32),
                   jax.ShapeDtypeStruct((2 * M_HEADS, tokens), F32),
                   jax.ShapeDtypeStruct((A_WIDTH, tokens), BF16),
                   jax.ShapeDtypeStruct((A_HEADS * VT_ROWS, tokens), BF16),
                   jax.ShapeDtypeStruct((M_WIDTH, tokens), BF16)],
        scratch_shapes=[pltpu.VMEM((tm + CARRY_ROWS, 2 * M_WIDTH), F32),
                        pltpu.VMEM((2, tm, M_WIDTH), F32),
                        pltpu.VMEM((d_model, ATT_GROUPS * M_WIDTH), BF16)],
        compiler_params=pltpu.CompilerParams(dimension_semantics=("arbitrary",),
                                             vmem_limit_bytes=PROJ_VMEM_LIMIT),
        name="proj",
    )(x2, attn_norm_g, w_bf, gate_b, q_gain, k_gain, ct, su, sd, conv_w, conv_b)


def _mlstm_chunk(r, q_ref, kt_ref, v_ref, o_ref, gc_ref, gr_ref, ng_ref, y_ref, state_ref, m_ref,
                 fresh):
    L = MLSTM_CHUNK
    D = M_HEAD_DIM
    H = M_HEADS
    rows = slice(r * L, (r + 1) * L)

    gc = gc_ref[rows, :]
    gr = gr_ref[:, rows]
    row = lax.broadcasted_iota(jnp.int32, (L, L), 0)
    col = lax.broadcasted_iota(jnp.int32, (L, L), 1)
    causal = col <= row
    lower = causal.astype(F32)
    upper = (row <= col).astype(F32)
    bc_all = jnp.dot(lower, gc, preferred_element_type=F32, precision=lax.Precision.HIGHEST)
    br_all = jnp.dot(gr, upper, preferred_element_type=F32, precision=lax.Precision.HIGHEST)
    ones_blk = jnp.ones((L, D), BF16)
    mean_mat = jnp.full((D, D), 1.0 / D, BF16)

    q = [q_ref[rows, h * D:(h + 1) * D] for h in range(H)]
    v_aug = [jnp.concatenate([v_ref[rows, h * D:(h + 1) * D], ones_blk], axis=1)
             for h in range(H)]
    b_c = [bc_all[:, H + h:H + h + 1] for h in range(H)]
    b_r = [br_all[H + h:H + h + 1, :] for h in range(H)]
    li_r = [gr[h:h + 1, :] for h in range(H)]
    if fresh is None:
        m_prev = [m_ref[h:h + 1, 0:1] for h in range(H)]
        c_state = [state_ref[h] for h in range(H)]
    else:
        m_prev = [jnp.where(fresh, 0.0, m_ref[h:h + 1, 0:1]) for h in range(H)]
        c_state = [jnp.where(fresh, 0.0, state_ref[h]) for h in range(H)]

    s = [jnp.dot(q[h], kt_ref[h * D:(h + 1) * D, rows], preferred_element_type=F32)
         for h in range(H)]
    inter = [jnp.dot(q[h], c_state[h].astype(BF16), preferred_element_type=F32)
             for h in range(H)]
    yield

    b_last = [b_r[h][:, L - 1:L] for h in range(H)]
    g_r = [b_last[h] - b_r[h] + li_r[h] for h in range(H)]
    m_new = [jnp.maximum(b_last[h] + m_prev[h], jnp.max(g_r[h], axis=-1, keepdims=True))
             for h in range(H)]
    kt_scaled = [(kt_ref[h * D:(h + 1) * D, rows].astype(F32)
                  * jnp.exp(g_r[h] - m_new[h])).astype(BF16) for h in range(H)]
    for h in range(H):
        decay = jnp.exp(b_last[h] + m_prev[h] - m_new[h])
        state_ref[h] = decay * c_state[h] + jnp.dot(kt_scaled[h], v_aug[h],
                                                    preferred_element_type=F32)
        m_ref[h:h + 1, :] = jnp.broadcast_to(m_new[h], (1, LANES))
    yield

    e = [jnp.where(causal, li_r[h] - b_r[h], NEG_BIG) for h in range(H)]
    u = [jnp.maximum(m_prev[h], jnp.max(e[h], axis=-1, keepdims=True)) for h in range(H)]
    u_rep = [jnp.broadcast_to(u[h], (L, D)) for h in range(H)]
    smat = [(s[h] * jnp.exp(e[h] - jnp.concatenate([u_rep[h]] * (L // D), axis=1))).astype(BF16)
            for h in range(H)]
    intra = [jnp.dot(smat[h], v_aug[h], preferred_element_type=F32) for h in range(H)]
    yield

    w_rep = [jnp.exp(m_prev[h] - u_rep[h]) for h in range(H)]
    num = [w_rep[h] * inter[h][:, :D] + intra[h][:, :D] for h in range(H)]
    den = [w_rep[h] * inter[h][:, D:] + intra[h][:, D:] for h in range(H)]
    floor_rep = [jnp.broadcast_to(jnp.exp(-(b_c[h] + u[h])), (L, D)) for h in range(H)]
    msq = [jnp.dot((num[h] * num[h]).astype(BF16), mean_mat, preferred_element_type=F32)
           for h in range(H)]
    for h in range(H):
        hs = slice(h * D, (h + 1) * D)
        rcp = 1.0 / jnp.maximum(jnp.abs(den[h]), floor_rep[h])
        f = rcp * lax.rsqrt(rcp * rcp * msq[h] + EPS)
        y_ref[rows, hs] = (jax.nn.sigmoid(o_ref[rows, hs].astype(F32)) * (num[h] * f)
                           * ng_ref[:, hs]).astype(BF16)
    yield


def _attn_kernel(qt_ref, k_ref, vt_ref, lam_ref, sg_ref, y_ref, m_ref, acc_ref, s0_ref, s1_ref,
                 qz_ref):
    TQ = ATTN_Q_BLOCK
    TK = ATTN_K_BLOCK
    D = A_HEAD_DIM
    i = pl.program_id(1)
    sub = lax.broadcasted_iota(jnp.int32, (D, 1), 0)
    for h in range(A_HEADS):
        qt = qt_ref[h * D:(h + 1) * D, :]
        zero = jnp.zeros_like(qt)
        qz_ref[h, :, 0:TQ] = jnp.where(sub < A_QK_DIM, qt, zero)
        qz_ref[h, :, TQ:2 * TQ] = jnp.where(sub >= A_QK_DIM, qt, zero)
    m_ref[...] = jnp.full_like(m_ref, NEG_BIG)
    acc_ref[...] = jnp.zeros_like(acc_ref)

    hi_cols = (slice(TK, TQ), slice(TQ + TK, 2 * TQ))

    def score_head(h, t, s_ref):
        start = pl.multiple_of(t * TK, TK)
        k = k_ref[pl.ds(start, TK), h * D:(h + 1) * D]
        s_ref[h] = jnp.dot(k, qz_ref[h], preferred_element_type=F32).astype(BF16)

    def update_head(h, t, s, cols=None):
        start = pl.multiple_of(t * TK, TK)
        m_old = m_ref[h:h + 1, :] if cols is None else \
            jnp.concatenate([m_ref[h:h + 1, c] for c in cols], axis=1)
        m_new = jnp.maximum(m_old, jnp.max(s, axis=0, keepdims=True).astype(F32))
        alpha = jnp.exp2(m_old - m_new)
        prob = jnp.exp2(s - m_new.astype(BF16))
        vt = vt_ref[h * VT_ROWS:(h + 1) * VT_ROWS, pl.ds(start, TK)]
        pv = jnp.dot(vt, prob, preferred_element_type=F32)
        if cols is None:
            m_ref[h:h + 1, :] = m_new
            acc_ref[h] = pv + alpha * acc_ref[h]
        else:
            for n, c in enumerate(cols):
                blk = slice(n * TK, (n + 1) * TK)
                m_ref[h:h + 1, c] = m_new[:, blk]
                acc_ref[h, :, c] = alpha[:, blk] * acc_ref[h, :, c] + pv[:, blk]

    for h in range(A_HEADS):
        score_head(h, 0, s0_ref)

    def body(jj, carry):
        for h in range(A_HEADS):
            score_head(h, 2 * jj + 1, s1_ref)
            update_head(h, 2 * jj, s0_ref[h])
        for h in range(A_HEADS):
            score_head(h, 2 * jj + 2, s0_ref)
            update_head(h, 2 * jj + 1, s1_ref[h])
        return carry

    lax.fori_loop(0, i, body, 0)

    visible = lax.broadcasted_iota(jnp.int32, (TK, TK), 0) <= \
        lax.broadcasted_iota(jnp.int32, (TK, TK), 1)
    start_b = pl.multiple_of((2 * i + 1) * TK, TK)
    for h in range(A_HEADS):
        k = k_ref[pl.ds(start_b, TK), h * D:(h + 1) * D]
        for n, c in enumerate(hi_cols):
            s1_ref[h, :, n * TK:(n + 1) * TK] = jnp.dot(
                k, qz_ref[h, :, c], preferred_element_type=F32).astype(BF16)

    def diag_first(h):
        s = s0_ref[h]
        parts = [s[:, n * TK:(n + 1) * TK] for n in range(2 * TQ // TK)]
        for n in (0, TQ // TK):
            parts[n] = jnp.where(visible, parts[n], NEG_BIG)
        return jnp.concatenate(parts, axis=1)

    def diag_second(h):
        return jnp.concatenate([jnp.where(visible, s1_ref[h, :, n * TK:(n + 1) * TK], NEG_BIG)
                                for n in range(len(hi_cols))], axis=1)

    for h in range(A_HEADS):
        update_head(h, 2 * i, diag_first(h))
    for h in range(A_HEADS):
        update_head(h, 2 * i + 1, diag_second(h), hi_cols)

    lq1, lk1, lq2, lk2 = lam_ref[0:1, :], lam_ref[1:2, :], lam_ref[2:3, :], lam_ref[3:4, :]
    lam = jnp.exp(jnp.sum(lq1 * lk1, axis=-1, keepdims=True)) \
        - jnp.exp(jnp.sum(lq2 * lk2, axis=-1, keepdims=True)) + LAMBDA_INIT
    for h in range(A_HEADS):
        hs = slice(h * D, (h + 1) * D)
        acc = acc_ref[h]
        ot = acc[:D, :] / acc[D:D + 1, :]
        o = (ot[:, :TQ] - lam * ot[:, TQ:]).T
        on = o * lax.rsqrt(jnp.mean(o * o, axis=-1, keepdims=True) + EPS) * sg_ref[:, hs]
        y_ref[:, hs] = (on * (1.0 - LAMBDA_INIT)).astype(BF16)


def _attention(z, q_t, v_t, lam_vecs, sub_g, batch, seq):
    TQ = ATTN_Q_BLOCK
    nq = seq // TQ
    tokens = batch * seq
    score_buf = pltpu.VMEM((A_HEADS, ATTN_K_BLOCK, 2 * TQ), BF16)
    return pl.pallas_call(
        _attn_kernel,
        grid=(batch, nq),
        in_specs=[pl.BlockSpec((A_WIDTH, TQ), lambda b, i: (0, b * nq + i)),
                  pl.BlockSpec((seq, A_WIDTH), lambda b, i: (b, COL_AK)),
                  pl.BlockSpec((A_HEADS * VT_ROWS, seq), lambda b, i: (0, b)),
                  _const_spec((4, LANES)), _const_spec((1, A_WIDTH))],
        out_specs=pl.BlockSpec((TQ, A_WIDTH), lambda b, i: (b * nq + i, 0)),
        out_shape=jax.ShapeDtypeStruct((tokens, A_WIDTH), BF16),
        scratch_shapes=[pltpu.VMEM((2 * A_HEADS, 2 * TQ), F32),
                        pltpu.VMEM((A_HEADS, VT_ROWS, 2 * TQ), F32),
                        score_buf, score_buf,
                        pltpu.VMEM((A_HEADS, A_HEAD_DIM, 2 * TQ), BF16)],
        compiler_params=pltpu.CompilerParams(dimension_semantics=("parallel", "arbitrary"),
                                             vmem_limit_bytes=VMEM_LIMIT),
        name="diff_attn",
    )(q_t, z, v_t, lam_vecs, sub_g)


def _rms(x, g):
    return x * lax.rsqrt(jnp.mean(x * x, axis=-1, keepdims=True) + EPS) * g


def _ffn_kernel(x_ref, ya_ref, p_ref, wo_ref, g2_ref, wu_ref, wd_ref, g3_ref, wg_ref, wp_ref,
                q0_ref, kt0_ref, v0_ref, o0_ref, gc0_ref, gr0_ref,
                q1_ref, kt1_ref, v1_ref, o1_ref, gc1_ref, gr1_ref, ng_ref,
                out_ref, ymc_ref, ymn_ref, state_ref, m_ref, *, blocks_per_seq):
    j = pl.program_id(0)
    chunks = x_ref.shape[0] // MLSTM_CHUNK
    cur = (q0_ref, kt0_ref, v0_ref, o0_ref, gc0_ref, gr0_ref, ng_ref)
    nxt = (q1_ref, kt1_ref, v1_ref, o1_ref, gc1_ref, gr1_ref, ng_ref)

    @pl.when(j == 0)
    def _():
        state_ref[...] = jnp.zeros_like(state_ref)
        m_ref[...] = jnp.zeros_like(m_ref)
        for r in range(chunks):
            for _ in _mlstm_chunk(r, *cur, ymn_ref, state_ref, m_ref, None):
                pass

    ymc_ref[...] = ymn_ref[...]
    next_fresh = (j + 1) % blocks_per_seq == 0

    def mlstm_phases():
        for r in range(chunks):
            yield from _mlstm_chunk(r, *nxt, ymn_ref, state_ref, m_ref,
                                    next_fresh if r == 0 else None)

    phases = mlstm_phases()

    def advance(n):
        for _ in range(n):
            next(phases, None)

    h = x_ref[...] \
        + jnp.dot(ymc_ref[...], wo_ref[0:M_WIDTH, :], preferred_element_type=F32) \
        + jnp.dot(ya_ref[...], wo_ref[M_WIDTH:, :], preferred_element_type=F32)
    advance(1)
    u2 = _rms(h, g2_ref[...]).astype(BF16)
    mlp = None
    for c in range(wu_ref.shape[1] // FF_CHUNK):
        cols = slice(c * FF_CHUNK, (c + 1) * FF_CHUNK)
        hid = jnp.maximum(jnp.dot(u2, wu_ref[:, cols], preferred_element_type=F32), 0.0)
        advance(2)
        part = jnp.dot((hid * hid).astype(BF16), wd_ref[cols, :], preferred_element_type=F32)
        advance(1)
        mlp = part if mlp is None else mlp + part
    h = h + mlp
    u3 = _rms(h, g3_ref[...]).astype(BF16)
    gate = jax.nn.sigmoid(jnp.dot(u3, wg_ref[...], preferred_element_type=F32))
    advance(2)
    e = jnp.dot(p_ref[...].astype(BF16), wp_ref[...], preferred_element_type=F32)
    advance(4 * chunks)
    out_ref[...] = h + gate * e


def _ffn(x2, y_a, p2, w_out, g2, w_up, w_down, g3, w_gate, w_ple, z, k_t, gates_c, gates_r,
         norm_g, seq):
    tokens, d_model = x2.shape
    tm = FFN_ROWS
    n = tokens // tm
    blk = M_WIDTH
    row_spec = lambda w: pl.BlockSpec((tm, w), lambda i: (i, 0))
    nxt = lambda i: jnp.minimum(i + 1, n - 1)
    mlstm_specs = lambda at: [
        pl.BlockSpec((tm, blk), lambda i: (at(i), COL_MQ)),
        pl.BlockSpec((blk, tm), lambda i: (0, at(i))),
        pl.BlockSpec((tm, blk), lambda i: (at(i), COL_MV)),
        pl.BlockSpec((tm, blk), lambda i: (at(i), COL_MO)),
        pl.BlockSpec((tm, LANES), lambda i: (at(i), 0)),
        pl.BlockSpec((2 * M_HEADS, tm), lambda i: (0, at(i)))]
    mlstm_args = [z, k_t, z, z, gates_c, gates_r]
    return pl.pallas_call(
        functools.partial(_ffn_kernel, blocks_per_seq=seq // tm),
        grid=(n,),
        in_specs=[row_spec(d_model), row_spec(A_WIDTH), row_spec(p2.shape[1]),
                  _const_spec(w_out.shape), _const_spec((1, d_model)), _const_spec(w_up.shape),
                  _const_spec(w_down.shape), _const_spec((1, d_model)),
                  _const_spec(w_gate.shape), _const_spec(w_ple.shape)]
        + mlstm_specs(lambda i: i) + mlstm_specs(nxt) + [_const_spec((1, blk))],
        out_specs=row_spec(d_model),
        out_shape=jax.ShapeDtypeStruct((tokens, d_model), F32),
        scratch_shapes=[pltpu.VMEM((tm, blk), BF16), pltpu.VMEM((tm, blk), BF16),
                        pltpu.VMEM((M_HEADS, M_HEAD_DIM, 2 * M_HEAD_DIM), F32),
                        pltpu.VMEM((2 * M_HEADS, LANES), F32)],
        compiler_params=pltpu.CompilerParams(dimension_semantics=("arbitrary",),
                                             vmem_limit_bytes=PROJ_VMEM_LIMIT),
        name="ffn",
    )(x2, y_a, p2, w_out, g2, w_up, w_down, g3, w_gate, w_ple, *mlstm_args, *mlstm_args, norm_g)


def _rope_tables(seq):
    pos = np.arange(seq, dtype=np.float32)
    inv_freq = (ROPE_THETA ** (-np.arange(0, ROPE_DIM, 2, dtype=np.float32) / ROPE_DIM)
                ).astype(np.float32)
    ang = pos[:, None] * inv_freq[None, :]
    cos, sin = np.cos(ang).astype(np.float32), np.sin(ang).astype(np.float32)
    pad = np.zeros((seq, A_QK_DIM - ROPE_DIM), np.float32)
    zero8 = np.zeros((seq, ROPE_HALF), np.float32)
    ct = np.concatenate([cos, cos, pad + 1.0], axis=1)
    su = np.concatenate([zero8, sin, pad], axis=1)
    sd = np.concatenate([-sin, zero8, pad], axis=1)
    rep = LANES // A_QK_DIM
    return tuple(jnp.asarray(np.tile(t, (1, rep))) for t in (ct, su, sd))


def kernel(x, p, attn_norm_g, w_in, conv_w, conv_b, igate_b, fgate_b, mlstm_norm_g, q_norm_g,
           k_norm_g, lambda_q1, lambda_k1, lambda_q2, lambda_k2, attn_sub_norm_g, w_out,
           mlp_norm_g, w_up, w_down, ple_norm_g, w_ple_gate, w_ple_proj):
    batch, seq, d_model = x.shape
    assert w_in.shape[0] == 1, "single-layer block"
    assert seq % PROJ_ROWS == 0 and seq % FFN_ROWS == 0 and seq % ATTN_Q_BLOCK == 0
    assert FFN_ROWS % MLSTM_CHUNK == 0
    tokens = batch * seq
    x2 = x.reshape(tokens, d_model).astype(F32)
    p2 = p.reshape(tokens, p.shape[-1])

    w_bf = w_in[0].astype(BF16)
    gate_b = jnp.pad(jnp.concatenate([igate_b[0], fgate_b[0]]),
                     (0, LANES - 2 * M_HEADS)).reshape(1, LANES).astype(F32)
    rep = LANES // A_QK_DIM
    q_gain = jnp.tile(q_norm_g[0].astype(F32), rep).reshape(1, LANES)
    k_gain = jnp.tile(k_norm_g[0].astype(F32), rep).reshape(1, LANES)
    ct, su, sd = _rope_tables(seq)

    z, gates_c, gates_r, q_t, v_t, k_t = _project(
        x2, attn_norm_g[0].reshape(1, d_model).astype(F32), w_bf, gate_b, q_gain,
        k_gain, ct, su, sd, conv_w[0].astype(F32), conv_b[0].reshape(1, -1).astype(F32), seq)

    lam_vecs = jnp.pad(jnp.stack([lambda_q1[0], lambda_k1[0], lambda_q2[0], lambda_k2[0]]),
                       ((0, 0), (0, LANES - A_QK_DIM))).astype(F32)
    y_a = _attention(z, q_t, v_t, lam_vecs, attn_sub_norm_g[0].reshape(1, -1).astype(F32), batch, seq)

    out = _ffn(x2, y_a, p2, w_out[0].astype(BF16),
               mlp_norm_g[0].reshape(1, d_model).astype(F32), w_up[0].astype(BF16),
               w_down[0].astype(BF16), ple_norm_g[0].reshape(1, d_model).astype(F32),
               w_ple_gate[0].astype(BF16), w_ple_proj[0].astype(BF16),
               z, k_t, gates_c, gates_r, mlstm_norm_g[0].reshape(1, -1).astype(F32), seq)
    return out.reshape(batch, seq, d_model).astype(x.dtype)
```

```python
import functools
import math

import jax
import jax.numpy as jnp
import numpy as np
from jax import lax
from jax.experimental import pallas as pl
from jax.experimental.pallas import tpu as pltpu

F32 = jnp.float32
BF16 = jnp.bfloat16

EPS = 1e-6
LANES = 128
M_HEADS = 4
M_HEAD_DIM = 128
M_WIDTH = M_HEADS * M_HEAD_DIM
CONV_WIDTH = 4
A_HEADS = 4
A_HEAD_DIM = 128
A_QK_DIM = 64
A_WIDTH = A_HEADS * A_HEAD_DIM
ROPE_THETA = 500000.0
ROPE_DIM = A_QK_DIM // 4
ROPE_HALF = ROPE_DIM // 2
LAMBDA_INIT = 0.8 - 0.6 * math.exp(-0.3 * 0)
LOG2E = math.log2(math.e)

COL_MQ, COL_MK, COL_MV, COL_MO, COL_AQ, COL_AK, COL_AV = range(7)
N_GROUPS = 7
Z_WIDTH = N_GROUPS * M_WIDTH
ATT_GROUPS = 3
GATE_COL0 = 4 * M_WIDTH
ATT_COL0 = GATE_COL0 + 2 * M_HEADS

PROJ_ROWS = 1024
MLSTM_CHUNK = 256
ATTN_K_BLOCK = 256
ATTN_Q_BLOCK = 2 * ATTN_K_BLOCK
FFN_ROWS = 512
FF_CHUNK = 1024
ONES_ROWS = 16
VT_ROWS = A_HEAD_DIM + ONES_ROWS
CARRY_ROWS = 8
NEG_BIG = -2.0 ** 100
VMEM_LIMIT = 56 * 1024 * 1024
BIG_VMEM_LIMIT = 60000 * 1024


def _const_spec(shape):
    return pl.BlockSpec(shape, lambda *_: (0,) * len(shape), pipeline_mode=pl.Buffered(1))


def _group_mean_square(sq, lo_mask):
    tot = jnp.sum(sq, axis=-1, keepdims=True)
    lo = jnp.sum(jnp.where(lo_mask, sq, 0.0), axis=-1, keepdims=True)
    return jnp.where(lo_mask, lo, tot - lo) * (1.0 / A_QK_DIM)


def _qk_norm_rope(z, gain, ct, su, sd, scale):
    lane = lax.broadcasted_iota(jnp.int32, (1, LANES), 1)
    lo_mask = lane < A_QK_DIM
    outs = []
    for j in range(z.shape[1] // LANES):
        blk = z[:, j * LANES:(j + 1) * LANES]
        ms = _group_mean_square(blk * blk, lo_mask)
        xg = blk * lax.rsqrt(ms + EPS) * gain
        up = pltpu.roll(xg, ROPE_HALF, axis=1)
        dn = pltpu.roll(xg, LANES - ROPE_HALF, axis=1)
        outs.append((xg * ct + up * su + dn * sd) * scale)
    return jnp.concatenate(outs, axis=1)


def _log_sigmoid(x):
    return jnp.minimum(x, 0.0) - jnp.log1p(jnp.exp(-jnp.abs(x)))


def _conv_silu(xbuf_ref, cols, rows, cw, cb):
    conv = cb + cw[CONV_WIDTH - 1:CONV_WIDTH, :] * xbuf_ref[CARRY_ROWS:, cols]
    for back in range(1, CONV_WIDTH):
        tap = CONV_WIDTH - 1 - back
        conv = conv + cw[tap:tap + 1, :] * xbuf_ref[pl.ds(CARRY_ROWS - back, rows), cols]
    xbuf_ref[0:CARRY_ROWS, cols] = xbuf_ref[rows:rows + CARRY_ROWS, cols]
    return conv * jax.nn.sigmoid(conv)


PROJ_ORDER = (COL_AQ, COL_AK, COL_AV, COL_MQ, COL_MK, COL_MV, COL_MO)
STAGED = {COL_MK: None, COL_AQ: 0, COL_AK: 1, COL_AV: 0}


def _proj_kernel(x_ref, g_ref, w_ref, gb_ref, qg_ref, kg_ref, ct_ref, su_ref, sd_ref,
                 cw_ref, cb_ref, z_ref, gc_ref, gr_ref, qt_ref, vt_ref, kt_ref, xbuf_ref, zbuf_ref,
                 watt_ref, *, pos_blocks):
    rows = x_ref.shape[0]

    @pl.when(pl.program_id(0) % pos_blocks == 0)
    def _():
        xbuf_ref[0:CARRY_ROWS, :] = jnp.zeros((CARRY_ROWS, 2 * M_WIDTH), F32)

    @pl.when(pl.program_id(0) == 0)
    def _():
        for j in range(ATT_GROUPS):
            watt_ref[:, j * M_WIDTH:(j + 1) * M_WIDTH] = \
                w_ref[:, ATT_COL0 + j * M_WIDTH:ATT_COL0 + (j + 1) * M_WIDTH]

    x = x_ref[...]
    ms = jnp.mean(x * x, axis=-1, keepdims=True)
    u = (x * lax.rsqrt(ms + EPS) * g_ref[...]).astype(BF16)

    def stage(c):
        cols = slice(c * M_WIDTH, (c + 1) * M_WIDTH)
        w = w_ref[:, cols] if c < COL_AQ else \
            watt_ref[:, (c - COL_AQ) * M_WIDTH:(c - COL_AQ + 1) * M_WIDTH]
        zc = jnp.dot(u, w, preferred_element_type=F32)
        if c == COL_MK:
            xbuf_ref[CARRY_ROWS:, cols] = zc
        elif c in STAGED:
            zbuf_ref[STAGED[c]] = zc
        else:
            z_ref[:, cols] = zc.astype(BF16)

    def epilogue(c):
        cols = slice(c * M_WIDTH, (c + 1) * M_WIDTH)
        if c == COL_MK:
            zc = _conv_silu(xbuf_ref, cols, rows, cw_ref[:, cols], cb_ref[:, cols])
        elif c == COL_AQ:
            zc = _qk_norm_rope(zbuf_ref[STAGED[c]], qg_ref[...], ct_ref[...], su_ref[...],
                               sd_ref[...], A_QK_DIM ** -0.5 * LOG2E)
        elif c == COL_AK:
            zc = _qk_norm_rope(zbuf_ref[STAGED[c]], kg_ref[...], ct_ref[...], su_ref[...],
                               sd_ref[...], 1.0)
        elif c == COL_AV:
            zc = zbuf_ref[STAGED[c]]
        else:
            return
        z_ref[:, cols] = zc.astype(BF16)
        if c == COL_MK:
            kt_ref[...] = zc.T.astype(BF16)
        elif c == COL_AQ:
            qt_ref[...] = zc.T.astype(BF16)
        elif c == COL_AV:
            zt = zc.T.astype(BF16)
            for h in range(A_HEADS):
                vt_ref[h * VT_ROWS:h * VT_ROWS + A_HEAD_DIM, :] = \
                    zt[h * A_HEAD_DIM:(h + 1) * A_HEAD_DIM, :]
                vt_ref[h * VT_ROWS + A_HEAD_DIM:(h + 1) * VT_ROWS, :] = \
                    jnp.ones((ONES_ROWS, rows), BF16)

    stage(PROJ_ORDER[0])
    for prev, cur in zip(PROJ_ORDER[:-1], PROJ_ORDER[1:]):
        stage(cur)
        epilogue(prev)
    gz = jnp.dot(u, w_ref[:, GATE_COL0:GATE_COL0 + LANES], preferred_element_type=F32) + gb_ref[...]
    epilogue(PROJ_ORDER[-1])
    lane = lax.broadcasted_iota(jnp.int32, (1, LANES), 1)
    gz = jnp.where(lane >= M_HEADS, _log_sigmoid(gz), gz)
    gc_ref[...] = gz
    gr_ref[...] = gz.T[:2 * M_HEADS, :]


def _project(x2, attn_norm_g, w_bf, gate_b, q_gain, k_gain, ct, su, sd, conv_w, conv_b, seq):
    tokens = x2.shape[0]
    d_model = x2.shape[1]
    tm = PROJ_ROWS
    pos_blocks = seq // tm
    row_spec = lambda w: pl.BlockSpec((tm, w), lambda i: (i, 0))
    tab_spec = pl.BlockSpec((tm, LANES), lambda i: (i % pos_blocks, 0))
    return pl.pallas_call(
        functools.partial(_proj_kernel, pos_blocks=pos_blocks),
        grid=(tokens // tm,),
        in_specs=[row_spec(d_model), _const_spec((1, d_model)), _const_spec(w_bf.shape),
                  _const_spec((1, LANES)), _const_spec((1, LANES)),
                  _const_spec((1, LANES)), tab_spec, tab_spec, tab_spec,
                  _const_spec((CONV_WIDTH, 2 * M_WIDTH)), _const_spec((1, 2 * M_WIDTH))],
        out_specs=[row_spec(Z_WIDTH), row_spec(LANES),
                   pl.BlockSpec((2 * M_HEADS, tm), lambda i: (0, i)),
                   pl.BlockSpec((A_WIDTH, tm), lambda i: (0, i)),
                   pl.BlockSpec((A_HEADS * VT_ROWS, tm), lambda i: (0, i)),
                   pl.BlockSpec((M_WIDTH, tm), lambda i: (0, i))],
        out_shape=[jax.ShapeDtypeStruct((tokens, Z_WIDTH), BF16),
                   jax.ShapeDtypeStruct((tokens, LANES), F32),
                   jax.ShapeDtypeStruct((2 * M_HEADS, tokens), F32),
                   jax.ShapeDtypeStruct((A_WIDTH, tokens), BF16),
                   jax.ShapeDtypeStruct((A_HEADS * VT_ROWS, tokens), BF16),
                   jax.ShapeDtypeStruct((M_WIDTH, tokens), BF16)],
        scratch_shapes=[pltpu.VMEM((tm + CARRY_ROWS, 2 * M_WIDTH), F32),
                        pltpu.VMEM((2, tm, M_WIDTH), F32),
                        pltpu.VMEM((d_model, ATT_GROUPS * M_WIDTH), BF16)],
        compiler_params=pltpu.CompilerParams(dimension_semantics=("arbitrary",),
                                             vmem_limit_bytes=BIG_VMEM_LIMIT),
        name="proj",
    )(x2, attn_norm_g, w_bf, gate_b, q_gain, k_gain, ct, su, sd, conv_w, conv_b)


def _mlstm_chunk(r, q_ref, kt_ref, v_ref, o_ref, gc_ref, gr_ref, ng_ref, cw_ref, cb_ref, y_ref,
                 state_ref, m_ref, fresh, q_hist):
    L = MLSTM_CHUNK
    D = M_HEAD_DIM
    H = M_HEADS
    rows = slice(r * L, (r + 1) * L)

    gc = gc_ref[rows, :]
    gr = gr_ref[:, rows]
    row = lax.broadcasted_iota(jnp.int32, (L, L), 0)
    col = lax.broadcasted_iota(jnp.int32, (L, L), 1)
    causal = col <= row
    lower = causal.astype(F32)
    upper = (row <= col).astype(F32)
    bc_all = jnp.dot(lower, gc, preferred_element_type=F32, precision=lax.Precision.HIGHEST)
    br_all = jnp.dot(gr, upper, preferred_element_type=F32, precision=lax.Precision.HIGHEST)
    ones_blk = jnp.ones((L, D), BF16)
    mean_mat = jnp.full((D, D), 1.0 / D, BF16)

    if q_hist is None:
        q_hist = q_ref[r * L - CARRY_ROWS:r * L, :]
    ext = jnp.concatenate([q_hist.astype(F32), q_ref[rows, :].astype(F32)], axis=0)
    conv = cb_ref[...]
    for back in range(CONV_WIDTH):
        tap = CONV_WIDTH - 1 - back
        conv = conv + cw_ref[tap:tap + 1, :] * ext[CARRY_ROWS - back:CARRY_ROWS - back + L, :]
    q_all = (conv * jax.nn.sigmoid(conv) * D ** -0.5).astype(BF16)
    q = [q_all[:, h * D:(h + 1) * D] for h in range(H)]
    v_aug = [jnp.concatenate([v_ref[rows, h * D:(h + 1) * D], ones_blk], axis=1)
             for h in range(H)]
    b_c = [bc_all[:, H + h:H + h + 1] for h in range(H)]
    b_r = [br_all[H + h:H + h + 1, :] for h in range(H)]
    li_r = [gr[h:h + 1, :] for h in range(H)]
    if fresh is None:
        m_prev = [m_ref[h:h + 1, 0:1] for h in range(H)]
        c_state = [state_ref[h] for h in range(H)]
    else:
        m_prev = [jnp.where(fresh, 0.0, m_ref[h:h + 1, 0:1]) for h in range(H)]
        c_state = [jnp.where(fresh, 0.0, state_ref[h]) for h in range(H)]

    s = [jnp.dot(q[h], kt_ref[h * D:(h + 1) * D, rows], preferred_element_type=F32)
         for h in range(H)]
    inter = [jnp.dot(q[h], c_state[h].astype(BF16), preferred_element_type=F32)
             for h in range(H)]
    yield

    b_last = [b_r[h][:, L - 1:L] for h in range(H)]
    g_r = [b_last[h] - b_r[h] + li_r[h] for h in range(H)]
    m_new = [jnp.maximum(b_last[h] + m_prev[h], jnp.max(g_r[h], axis=-1, keepdims=True))
             for h in range(H)]
    kt_scaled = [(kt_ref[h * D:(h + 1) * D, rows].astype(F32)
                  * jnp.exp(g_r[h] - m_new[h])).astype(BF16) for h in range(H)]
    for h in range(H):
        decay = jnp.exp(b_last[h] + m_prev[h] - m_new[h])
        state_ref[h] = decay * c_state[h] + jnp.dot(kt_scaled[h], v_aug[h],
                                                    preferred_element_type=F32)
        m_ref[h:h + 1, :] = jnp.broadcast_to(m_new[h], (1, LANES))
    yield

    e = [jnp.where(causal, li_r[h] - b_r[h], NEG_BIG) for h in range(H)]
    u = [jnp.maximum(m_prev[h], jnp.max(e[h], axis=-1, keepdims=True)) for h in range(H)]
    u_rep = [jnp.broadcast_to(u[h], (L, D)) for h in range(H)]
    smat = [(s[h] * jnp.exp(e[h] - jnp.concatenate([u_rep[h]] * (L // D), axis=1))).astype(BF16)
            for h in range(H)]
    intra = [jnp.dot(smat[h], v_aug[h], preferred_element_type=F32) for h in range(H)]
    yield

    w_rep = [jnp.exp(m_prev[h] - u_rep[h]) for h in range(H)]
    num = [w_rep[h] * inter[h][:, :D] + intra[h][:, :D] for h in range(H)]
    den = [w_rep[h] * inter[h][:, D:] + intra[h][:, D:] for h in range(H)]
    floor_rep = [jnp.broadcast_to(jnp.exp(-(b_c[h] + u[h])), (L, D)) for h in range(H)]
    msq = [jnp.dot((num[h] * num[h]).astype(BF16), mean_mat, preferred_element_type=F32)
           for h in range(H)]
    for h in range(H):
        hs = slice(h * D, (h + 1) * D)
        rcp = 1.0 / jnp.maximum(jnp.abs(den[h]), floor_rep[h])
        f = rcp * lax.rsqrt(rcp * rcp * msq[h] + EPS)
        y_ref[rows, hs] = (jax.nn.sigmoid(o_ref[rows, hs].astype(F32)) * (num[h] * f)
                           * ng_ref[:, hs]).astype(BF16)
    yield


def _attn_kernel(qt_ref, k_ref, vt_ref, lam_ref, sg_ref, y_ref, m_ref, acc_ref, s0_ref, s1_ref,
                 qz_ref):
    TQ = ATTN_Q_BLOCK
    TK = ATTN_K_BLOCK
    D = A_HEAD_DIM
    i = pl.program_id(1)
    sub = lax.broadcasted_iota(jnp.int32, (D, 1), 0)
    for h in range(A_HEADS):
        qt = qt_ref[h * D:(h + 1) * D, :]
        zero = jnp.zeros_like(qt)
        qz_ref[h, :, 0:TQ] = jnp.where(sub < A_QK_DIM, qt, zero)
        qz_ref[h, :, TQ:2 * TQ] = jnp.where(sub >= A_QK_DIM, qt, zero)
    m_ref[...] = jnp.full_like(m_ref, NEG_BIG)
    acc_ref[...] = jnp.zeros_like(acc_ref)

    hi_cols = (slice(TK, TQ), slice(TQ + TK, 2 * TQ))

    def score_head(h, t, s_ref):
        start = pl.multiple_of(t * TK, TK)
        k = k_ref[pl.ds(start, TK), h * D:(h + 1) * D]
        s_ref[h] = jnp.dot(k, qz_ref[h], preferred_element_type=F32).astype(BF16)

    def update_head(h, t, s, cols=None):
        start = pl.multiple_of(t * TK, TK)
        m_old = m_ref[h:h + 1, :] if cols is None else \
            jnp.concatenate([m_ref[h:h + 1, c] for c in cols], axis=1)
        m_new = jnp.maximum(m_old, jnp.max(s, axis=0, keepdims=True).astype(F32))
        alpha = jnp.exp2(m_old - m_new)
        prob = jnp.exp2(s - m_new.astype(BF16))
        vt = vt_ref[h * VT_ROWS:(h + 1) * VT_ROWS, pl.ds(start, TK)]
        pv = jnp.dot(vt, prob, preferred_element_type=F32)
        if cols is None:
            m_ref[h:h + 1, :] = m_new
            acc_ref[h] = pv + alpha * acc_ref[h]
        else:
            for n, c in enumerate(cols):
                blk = slice(n * TK, (n + 1) * TK)
                m_ref[h:h + 1, c] = m_new[:, blk]
                acc_ref[h, :, c] = alpha[:, blk] * acc_ref[h, :, c] + pv[:, blk]

    for h in range(A_HEADS):
        score_head(h, 0, s0_ref)

    def body(jj, carry):
        for h in range(A_HEADS):
            score_head(h, 2 * jj + 1, s1_ref)
            update_head(h, 2 * jj, s0_ref[h])
        for h in range(A_HEADS):
            score_head(h, 2 * jj + 2, s0_ref)
            update_head(h, 2 * jj + 1, s1_ref[h])
        return carry

    lax.fori_loop(0, i, body, 0)

    visible = lax.broadcasted_iota(jnp.int32, (TK, TK), 0) <= \
        lax.broadcasted_iota(jnp.int32, (TK, TK), 1)
    start_b = pl.multiple_of((2 * i + 1) * TK, TK)
    for h in range(A_HEADS):
        k = k_ref[pl.ds(start_b, TK), h * D:(h + 1) * D]
        for n, c in enumerate(hi_cols):
            s1_ref[h, :, n * TK:(n + 1) * TK] = jnp.dot(
                k, qz_ref[h, :, c], preferred_element_type=F32).astype(BF16)

    def diag_first(h):
        s = s0_ref[h]
        parts = [s[:, n * TK:(n + 1) * TK] for n in range(2 * TQ // TK)]
        for n in (0, TQ // TK):
            parts[n] = jnp.where(visible, parts[n], NEG_BIG)
        return jnp.concatenate(parts, axis=1)

    def diag_second(h):
        return jnp.concatenate([jnp.where(visible, s1_ref[h, :, n * TK:(n + 1) * TK], NEG_BIG)
                                for n in range(len(hi_cols))], axis=1)

    for h in range(A_HEADS):
        update_head(h, 2 * i, diag_first(h))
    for h in range(A_HEADS):
        update_head(h, 2 * i + 1, diag_second(h), hi_cols)

    lq1, lk1, lq2, lk2 = lam_ref[0:1, :], lam_ref[1:2, :], lam_ref[2:3, :], lam_ref[3:4, :]
    lam = jnp.exp(jnp.sum(lq1 * lk1, axis=-1, keepdims=True)) \
        - jnp.exp(jnp.sum(lq2 * lk2, axis=-1, keepdims=True)) + LAMBDA_INIT
    for h in range(A_HEADS):
        hs = slice(h * D, (h + 1) * D)
        acc = acc_ref[h]
        ot = acc[:D, :] / acc[D:D + 1, :]
        o = (ot[:, :TQ] - lam * ot[:, TQ:]).T
        on = o * lax.rsqrt(jnp.mean(o * o, axis=-1, keepdims=True) + EPS) * sg_ref[:, hs]
        y_ref[:, hs] = (on * (1.0 - LAMBDA_INIT)).astype(BF16)


def _attention(z, q_t, v_t, lam_vecs, sub_g, batch, seq):
    TQ = ATTN_Q_BLOCK
    nq = seq // TQ
    tokens = batch * seq
    score_buf = pltpu.VMEM((A_HEADS, ATTN_K_BLOCK, 2 * TQ), BF16)
    return pl.pallas_call(
        _attn_kernel,
        grid=(batch, nq),
        in_specs=[pl.BlockSpec((A_WIDTH, TQ), lambda b, i: (0, b * nq + i)),
                  pl.BlockSpec((seq, A_WIDTH), lambda b, i: (b, COL_AK)),
                  pl.BlockSpec((A_HEADS * VT_ROWS, seq), lambda b, i: (0, b)),
                  _const_spec((4, LANES)), _const_spec((1, A_WIDTH))],
        out_specs=pl.BlockSpec((TQ, A_WIDTH), lambda b, i: (b * nq + i, 0)),
        out_shape=jax.ShapeDtypeStruct((tokens, A_WIDTH), BF16),
        scratch_shapes=[pltpu.VMEM((2 * A_HEADS, 2 * TQ), F32),
                        pltpu.VMEM((A_HEADS, VT_ROWS, 2 * TQ), F32),
                        score_buf, score_buf,
                        pltpu.VMEM((A_HEADS, A_HEAD_DIM, 2 * TQ), BF16)],
        compiler_params=pltpu.CompilerParams(dimension_semantics=("parallel", "arbitrary"),
                                             vmem_limit_bytes=VMEM_LIMIT),
        name="diff_attn",
    )(q_t, z, v_t, lam_vecs, sub_g)


def _rms(x, g):
    return x * lax.rsqrt(jnp.mean(x * x, axis=-1, keepdims=True) + EPS) * g


def _ffn_kernel(x_ref, ya_ref, p_ref, wo_ref, g2_ref, wu_ref, wd_ref, g3_ref, wg_ref, wp_ref,
                q0_ref, kt0_ref, v0_ref, o0_ref, gc0_ref, gr0_ref,
                q1_ref, kt1_ref, v1_ref, o1_ref, gc1_ref, gr1_ref, ng_ref, cw_ref, cb_ref,
                out_ref, ymc_ref, ymn_ref, state_ref, m_ref, *, blocks_per_seq):
    j = pl.program_id(0)
    chunks = x_ref.shape[0] // MLSTM_CHUNK
    cur = (q0_ref, kt0_ref, v0_ref, o0_ref, gc0_ref, gr0_ref, ng_ref, cw_ref, cb_ref)
    nxt = (q1_ref, kt1_ref, v1_ref, o1_ref, gc1_ref, gr1_ref, ng_ref, cw_ref, cb_ref)
    zero_hist = jnp.zeros((CARRY_ROWS, M_WIDTH), BF16)

    @pl.when(j == 0)
    def _():
        state_ref[...] = jnp.zeros_like(state_ref)
        m_ref[...] = jnp.zeros_like(m_ref)
        for r in range(chunks):
            for _ in _mlstm_chunk(r, *cur, ymn_ref, state_ref, m_ref, None,
                                  zero_hist if r == 0 else None):
                pass

    ymc_ref[...] = ymn_ref[...]
    next_fresh = (j + 1) % blocks_per_seq == 0

    def mlstm_phases():
        for r in range(chunks):
            hist = jnp.where(next_fresh, zero_hist, q0_ref[x_ref.shape[0] - CARRY_ROWS:, :])
            yield from _mlstm_chunk(r, *nxt, ymn_ref, state_ref, m_ref,
                                    next_fresh if r == 0 else None, hist if r == 0 else None)

    phases = mlstm_phases()

    def advance(n):
        for _ in range(n):
            next(phases, None)

    h = x_ref[...] \
        + jnp.dot(ymc_ref[...], wo_ref[0:M_WIDTH, :], preferred_element_type=F32) \
        + jnp.dot(ya_ref[...], wo_ref[M_WIDTH:, :], preferred_element_type=F32)
    advance(1)
    u2 = _rms(h, g2_ref[...]).astype(BF16)
    mlp = None
    for c in range(wu_ref.shape[1] // FF_CHUNK):
        cols = slice(c * FF_CHUNK, (c + 1) * FF_CHUNK)
        hid = jnp.maximum(jnp.dot(u2, wu_ref[:, cols], preferred_element_type=F32), 0.0)
        advance(2)
        part = jnp.dot((hid * hid).astype(BF16), wd_ref[cols, :], preferred_element_type=F32)
        advance(1)
        mlp = part if mlp is None else mlp + part
    h = h + mlp
    u3 = _rms(h, g3_ref[...]).astype(BF16)
    gate = jax.nn.sigmoid(jnp.dot(u3, wg_ref[...], preferred_element_type=F32))
    advance(2)
    e = jnp.dot(p_ref[...].astype(BF16), wp_ref[...], preferred_element_type=F32)
    advance(4 * chunks)
    out_ref[...] = h + gate * e


def _ffn(x2, y_a, p2, w_out, g2, w_up, w_down, g3, w_gate, w_ple, z, k_t, gates_c, gates_r,
         norm_g, conv_w, conv_b, seq):
    tokens, d_model = x2.shape
    tm = FFN_ROWS
    n = tokens // tm
    blk = M_WIDTH
    row_spec = lambda w: pl.BlockSpec((tm, w), lambda i: (i, 0))
    nxt = lambda i: jnp.minimum(i + 1, n - 1)
    mlstm_specs = lambda at: [
        pl.BlockSpec((tm, blk), lambda i: (at(i), COL_MQ)),
        pl.BlockSpec((blk, tm), lambda i: (0, at(i))),
        pl.BlockSpec((tm, blk), lambda i: (at(i), COL_MV)),
        pl.BlockSpec((tm, blk), lambda i: (at(i), COL_MO)),
        pl.BlockSpec((tm, LANES), lambda i: (at(i), 0)),
        pl.BlockSpec((2 * M_HEADS, tm), lambda i: (0, at(i)))]
    mlstm_args = [z, k_t, z, z, gates_c, gates_r]
    return pl.pallas_call(
        functools.partial(_ffn_kernel, blocks_per_seq=seq // tm),
        grid=(n,),
        in_specs=[row_spec(d_model), row_spec(A_WIDTH), row_spec(p2.shape[1]),
                  _const_spec(w_out.shape), _const_spec((1, d_model)), _const_spec(w_up.shape),
                  _const_spec(w_down.shape), _const_spec((1, d_model)),
                  _const_spec(w_gate.shape), _const_spec(w_ple.shape)]
        + mlstm_specs(lambda i: i) + mlstm_specs(nxt)
        + [_const_spec((1, blk)), _const_spec((CONV_WIDTH, blk)), _const_spec((1, blk))],
        out_specs=row_spec(d_model),
        out_shape=jax.ShapeDtypeStruct((tokens, d_model), F32),
        scratch_shapes=[pltpu.VMEM((tm, blk), BF16), pltpu.VMEM((tm, blk), BF16),
                        pltpu.VMEM((M_HEADS, M_HEAD_DIM, 2 * M_HEAD_DIM), F32),
                        pltpu.VMEM((2 * M_HEADS, LANES), F32)],
        compiler_params=pltpu.CompilerParams(dimension_semantics=("arbitrary",),
                                             vmem_limit_bytes=BIG_VMEM_LIMIT),
        name="ffn",
    )(x2, y_a, p2, w_out, g2, w_up, w_down, g3, w_gate, w_ple, *mlstm_args, *mlstm_args, norm_g,
      conv_w[:, :blk], conv_b[:, :blk])


def _rope_tables(seq):
    pos = np.arange(seq, dtype=np.float32)
    inv_freq = (ROPE_THETA ** (-np.arange(0, ROPE_DIM, 2, dtype=np.float32) / ROPE_DIM)
                ).astype(np.float32)
    ang = pos[:, None] * inv_freq[None, :]
    cos, sin = np.cos(ang).astype(np.float32), np.sin(ang).astype(np.float32)
    pad = np.zeros((seq, A_QK_DIM - ROPE_DIM), np.float32)
    zero8 = np.zeros((seq, ROPE_HALF), np.float32)
    ct = np.concatenate([cos, cos, pad + 1.0], axis=1)
    su = np.concatenate([zero8, sin, pad], axis=1)
    sd = np.concatenate([-sin, zero8, pad], axis=1)
    rep = LANES // A_QK_DIM
    return tuple(jnp.asarray(np.tile(t, (1, rep))) for t in (ct, su, sd))


def kernel(x, p, attn_norm_g, w_in, conv_w, conv_b, igate_b, fgate_b, mlstm_norm_g, q_norm_g,
           k_norm_g, lambda_q1, lambda_k1, lambda_q2, lambda_k2, attn_sub_norm_g, w_out,
           mlp_norm_g, w_up, w_down, ple_norm_g, w_ple_gate, w_ple_proj):
    batch, seq, d_model = x.shape
    assert w_in.shape[0] == 1, "single-layer block"
    assert seq % PROJ_ROWS == 0 and seq % FFN_ROWS == 0 and seq % ATTN_Q_BLOCK == 0
    assert FFN_ROWS % MLSTM_CHUNK == 0
    tokens = batch * seq
    x2 = x.reshape(tokens, d_model).astype(F32)
    p2 = p.reshape(tokens, p.shape[-1])

    w_bf = w_in[0].astype(BF16)
    gate_b = jnp.pad(jnp.concatenate([igate_b[0], fgate_b[0]]),
                     (0, LANES - 2 * M_HEADS)).reshape(1, LANES).astype(F32)
    rep = LANES // A_QK_DIM
    q_gain = jnp.tile(q_norm_g[0].astype(F32), rep).reshape(1, LANES)
    k_gain = jnp.tile(k_norm_g[0].astype(F32), rep).reshape(1, LANES)
    ct, su, sd = _rope_tables(seq)

    z, gates_c, gates_r, q_t, v_t, k_t = _project(
        x2, attn_norm_g[0].reshape(1, d_model).astype(F32), w_bf, gate_b, q_gain,
        k_gain, ct, su, sd, conv_w[0].astype(F32), conv_b[0].reshape(1, -1).astype(F32), seq)

    lam_vecs = jnp.pad(jnp.stack([lambda_q1[0], lambda_k1[0], lambda_q2[0], lambda_k2[0]]),
                       ((0, 0), (0, LANES - A_QK_DIM))).astype(F32)
    y_a = _attention(z, q_t, v_t, lam_vecs, attn_sub_norm_g[0].reshape(1, -1).astype(F32), batch, seq)

    out = _ffn(x2, y_a, p2, w_out[0].astype(BF16),
               mlp_norm_g[0].reshape(1, d_model).astype(F32), w_up[0].astype(BF16),
               w_down[0].astype(BF16), ple_norm_g[0].reshape(1, d_model).astype(F32),
               w_ple_gate[0].astype(BF16), w_ple_proj[0].astype(BF16),
               z, k_t, gates_c, gates_r, mlstm_norm_g[0].reshape(1, -1).astype(F32),
               conv_w[0].astype(F32), conv_b[0].reshape(1, -1).astype(F32), seq)
    return out.reshape(batch, seq, d_model).astype(x.dtype)
```

```python
import functools
import math

import jax
import jax.numpy as jnp
import numpy as np
from jax import lax
from jax.experimental import pallas as pl
from jax.experimental.pallas import tpu as pltpu

F32 = jnp.float32
BF16 = jnp.bfloat16

EPS = 1e-6
LANES = 128
M_HEADS = 4
M_HEAD_DIM = 128
M_WIDTH = M_HEADS * M_HEAD_DIM
CONV_WIDTH = 4
A_HEADS = 4
A_HEAD_DIM = 128
A_QK_DIM = 64
A_WIDTH = A_HEADS * A_HEAD_DIM
ROPE_THETA = 500000.0
ROPE_DIM = A_QK_DIM // 4
ROPE_HALF = ROPE_DIM // 2
LAMBDA_INIT = 0.8 - 0.6 * math.exp(-0.3 * 0)
LOG2E = math.log2(math.e)

COL_MQ, COL_MK, COL_MV, COL_MO, COL_AQ, COL_AK, COL_AV = range(7)
N_GROUPS = 7
Z_WIDTH = N_GROUPS * M_WIDTH
ATT_GROUPS = 3
GATE_COL0 = 4 * M_WIDTH
ATT_COL0 = GATE_COL0 + 2 * M_HEADS

PROJ_ROWS = 1024
MLSTM_CHUNK = 256
ATTN_K_BLOCK = 256
ATTN_Q_BLOCK = 2 * ATTN_K_BLOCK
FFN_ROWS = 512
FF_CHUNK = 1024
ONES_ROWS = 16
VT_ROWS = A_HEAD_DIM + ONES_ROWS
CARRY_ROWS = 8
NEG_BIG = -2.0 ** 100
VMEM_LIMIT = 56 * 1024 * 1024
BIG_VMEM_LIMIT = 60000 * 1024


def _const_spec(shape):
    return pl.BlockSpec(shape, lambda *_: (0,) * len(shape), pipeline_mode=pl.Buffered(1))


def _group_mean_square(sq, lo_mask):
    tot = jnp.sum(sq, axis=-1, keepdims=True)
    lo = jnp.sum(jnp.where(lo_mask, sq, 0.0), axis=-1, keepdims=True)
    return jnp.where(lo_mask, lo, tot - lo) * (1.0 / A_QK_DIM)


def _qk_norm_rope(z, gain, ct, su, sd, scale):
    lane = lax.broadcasted_iota(jnp.int32, (1, LANES), 1)
    lo_mask = lane < A_QK_DIM
    outs = []
    for j in range(z.shape[1] // LANES):
        blk = z[:, j * LANES:(j + 1) * LANES]
        ms = _group_mean_square(blk * blk, lo_mask)
        xg = blk * lax.rsqrt(ms + EPS) * gain
        up = pltpu.roll(xg, ROPE_HALF, axis=1)
        dn = pltpu.roll(xg, LANES - ROPE_HALF, axis=1)
        outs.append((xg * ct + up * su + dn * sd) * scale)
    return jnp.concatenate(outs, axis=1)


def _log_sigmoid(x):
    return jnp.minimum(x, 0.0) - jnp.log1p(jnp.exp(-jnp.abs(x)))


def _conv_silu(xbuf_ref, cols, rows, cw, cb):
    conv = cb + cw[CONV_WIDTH - 1:CONV_WIDTH, :] * xbuf_ref[CARRY_ROWS:, cols]
    for back in range(1, CONV_WIDTH):
        tap = CONV_WIDTH - 1 - back
        conv = conv + cw[tap:tap + 1, :] * xbuf_ref[pl.ds(CARRY_ROWS - back, rows), cols]
    xbuf_ref[0:CARRY_ROWS, cols] = xbuf_ref[rows:rows + CARRY_ROWS, cols]
    return conv * jax.nn.sigmoid(conv)


EPILOGUE_SLAB = 256
PROJ_ORDER = (COL_AQ, COL_AK, COL_MK, COL_AV, COL_MQ, COL_MV, COL_MO)
STAGED = {COL_MK: None, COL_AQ: 0, COL_AK: 1, COL_AV: 0}


def _proj_kernel(x_ref, g_ref, w_ref, gb_ref, qg_ref, kg_ref, ct_ref, su_ref, sd_ref,
                 cw_ref, cb_ref, z_ref, gc_ref, gr_ref, qt_ref, vt_ref, kt_ref, xbuf_ref, zbuf_ref,
                 watt_ref, *, pos_blocks):
    rows = x_ref.shape[0]

    @pl.when(pl.program_id(0) % pos_blocks == 0)
    def _():
        xbuf_ref[0:CARRY_ROWS, :] = jnp.zeros((CARRY_ROWS, 2 * M_WIDTH), F32)

    @pl.when(pl.program_id(0) == 0)
    def _():
        for j in range(ATT_GROUPS):
            watt_ref[:, j * M_WIDTH:(j + 1) * M_WIDTH] = \
                w_ref[:, ATT_COL0 + j * M_WIDTH:ATT_COL0 + (j + 1) * M_WIDTH]

    x = x_ref[...]
    ms = jnp.mean(x * x, axis=-1, keepdims=True)
    u = (x * lax.rsqrt(ms + EPS) * g_ref[...]).astype(BF16)

    def stage(c):
        cols = slice(c * M_WIDTH, (c + 1) * M_WIDTH)
        w = w_ref[:, cols] if c < COL_AQ else \
            watt_ref[:, (c - COL_AQ) * M_WIDTH:(c - COL_AQ + 1) * M_WIDTH]
        zc = jnp.dot(u, w, preferred_element_type=F32)
        if c == COL_MK:
            xbuf_ref[CARRY_ROWS:, cols] = zc
        elif c in STAGED:
            zbuf_ref[STAGED[c]] = zc
        else:
            z_ref[:, cols] = zc.astype(BF16)

    def epilogue(c):
        cols = slice(c * M_WIDTH, (c + 1) * M_WIDTH)
        if c == COL_MK:
            zc = _conv_silu(xbuf_ref, cols, rows, cw_ref[:, cols], cb_ref[:, cols])
        elif c in (COL_AQ, COL_AK):
            gain = qg_ref[...] if c == COL_AQ else kg_ref[...]
            scale = A_QK_DIM ** -0.5 * LOG2E if c == COL_AQ else 1.0
            for r0 in range(0, rows, EPILOGUE_SLAB):
                rs = slice(r0, r0 + EPILOGUE_SLAB)
                part = _qk_norm_rope(zbuf_ref[STAGED[c], rs, :], gain, ct_ref[rs, :], su_ref[rs, :],
                                     sd_ref[rs, :], scale)
                z_ref[rs, cols] = part.astype(BF16)
                if c == COL_AQ:
                    qt_ref[:, rs] = part.T.astype(BF16)
            return
        elif c == COL_AV:
            zc = zbuf_ref[STAGED[c]]
        else:
            return
        z_ref[:, cols] = zc.astype(BF16)
        if c == COL_MK:
            kt_ref[...] = zc.T.astype(BF16)
        elif c == COL_AQ:
            qt_ref[...] = zc.T.astype(BF16)
        elif c == COL_AV:
            zt = zc.T.astype(BF16)
            for h in range(A_HEADS):
                vt_ref[h * VT_ROWS:h * VT_ROWS + A_HEAD_DIM, :] = \
                    zt[h * A_HEAD_DIM:(h + 1) * A_HEAD_DIM, :]
                vt_ref[h * VT_ROWS + A_HEAD_DIM:(h + 1) * VT_ROWS, :] = \
                    jnp.ones((ONES_ROWS, rows), BF16)

    stage(PROJ_ORDER[0])
    for prev, cur in zip(PROJ_ORDER[:-1], PROJ_ORDER[1:]):
        stage(cur)
        epilogue(prev)
    gz = jnp.dot(u, w_ref[:, GATE_COL0:GATE_COL0 + LANES], preferred_element_type=F32) + gb_ref[...]
    epilogue(PROJ_ORDER[-1])
    lane = lax.broadcasted_iota(jnp.int32, (1, LANES), 1)
    gz = jnp.where(lane >= M_HEADS, _log_sigmoid(gz), gz)
    gc_ref[...] = gz
    gr_ref[...] = gz.T[:2 * M_HEADS, :]


def _project(x2, attn_norm_g, w_bf, gate_b, q_gain, k_gain, ct, su, sd, conv_w, conv_b, seq):
    tokens = x2.shape[0]
    d_model = x2.shape[1]
    tm = PROJ_ROWS
    pos_blocks = seq // tm
    row_spec = lambda w: pl.BlockSpec((tm, w), lambda i: (i, 0))
    tab_spec = pl.BlockSpec((tm, LANES), lambda i: (i % pos_blocks, 0))
    return pl.pallas_call(
        functools.partial(_proj_kernel, pos_blocks=pos_blocks),
        grid=(tokens // tm,),
        in_specs=[row_spec(d_model), _const_spec((1, d_model)), _const_spec(w_bf.shape),
                  _const_spec((1, LANES)), _const_spec((1, LANES)),
                  _const_spec((1, LANES)), tab_spec, tab_spec, tab_spec,
                  _const_spec((CONV_WIDTH, 2 * M_WIDTH)), _const_spec((1, 2 * M_WIDTH))],
        out_specs=[row_spec(Z_WIDTH), row_spec(LANES),
                   pl.BlockSpec((2 * M_HEADS, tm), lambda i: (0, i)),
                   pl.BlockSpec((A_WIDTH, tm), lambda i: (0, i)),
                   pl.BlockSpec((A_HEADS * VT_ROWS, tm), lambda i: (0, i)),
                   pl.BlockSpec((M_WIDTH, tm), lambda i: (0, i))],
        out_shape=[jax.ShapeDtypeStruct((tokens, Z_WIDTH), BF16),
                   jax.ShapeDtypeStruct((tokens, LANES), F32),
                   jax.ShapeDtypeStruct((2 * M_HEADS, tokens), F32),
                   jax.ShapeDtypeStruct((A_WIDTH, tokens), BF16),
                   jax.ShapeDtypeStruct((A_HEADS * VT_ROWS, tokens), BF16),
                   jax.ShapeDtypeStruct((M_WIDTH, tokens), BF16)],
        scratch_shapes=[pltpu.VMEM((tm + CARRY_ROWS, 2 * M_WIDTH), F32),
                        pltpu.VMEM((2, tm, M_WIDTH), F32),
                        pltpu.VMEM((d_model, ATT_GROUPS * M_WIDTH), BF16)],
        compiler_params=pltpu.CompilerParams(dimension_semantics=("arbitrary",),
                                             vmem_limit_bytes=BIG_VMEM_LIMIT),
        name="proj",
    )(x2, attn_norm_g, w_bf, gate_b, q_gain, k_gain, ct, su, sd, conv_w, conv_b)


def _mlstm_chunk(r, q_ref, kt_ref, v_ref, o_ref, gc_ref, gr_ref, ng_ref, cw_ref, cb_ref, y_ref,
                 state_ref, m_ref, fresh, q_hist):
    L = MLSTM_CHUNK
    D = M_HEAD_DIM
    H = M_HEADS
    rows = slice(r * L, (r + 1) * L)

    gc = gc_ref[rows, :]
    gr = gr_ref[:, rows]
    row = lax.broadcasted_iota(jnp.int32, (L, L), 0)
    col = lax.broadcasted_iota(jnp.int32, (L, L), 1)
    causal = col <= row
    lower = causal.astype(F32)
    upper = (row <= col).astype(F32)
    bc_all = jnp.dot(lower, gc, preferred_element_type=F32, precision=lax.Precision.HIGHEST)
    br_all = jnp.dot(gr, upper, preferred_element_type=F32, precision=lax.Precision.HIGHEST)
    ones_blk = jnp.ones((L, D), BF16)
    mean_mat = jnp.full((D, D), 1.0 / D, BF16)

    if q_hist is None:
        q_hist = q_ref[r * L - CARRY_ROWS:r * L, :]
    ext = jnp.concatenate([q_hist.astype(F32), q_ref[rows, :].astype(F32)], axis=0)
    conv = cb_ref[...]
    for back in range(CONV_WIDTH):
        tap = CONV_WIDTH - 1 - back
        conv = conv + cw_ref[tap:tap + 1, :] * ext[CARRY_ROWS - back:CARRY_ROWS - back + L, :]
    q_all = (conv * jax.nn.sigmoid(conv) * D ** -0.5).astype(BF16)
    q = [q_all[:, h * D:(h + 1) * D] for h in range(H)]
    v_aug = [jnp.concatenate([v_ref[rows, h * D:(h + 1) * D], ones_blk], axis=1)
             for h in range(H)]
    b_c = [bc_all[:, H + h:H + h + 1] for h in range(H)]
    b_r = [br_all[H + h:H + h + 1, :] for h in range(H)]
    li_r = [gr[h:h + 1, :] for h in range(H)]
    if fresh is None:
        m_prev = [m_ref[h:h + 1, 0:1] for h in range(H)]
        c_state = [state_ref[h] for h in range(H)]
    else:
        m_prev = [jnp.where(fresh, 0.0, m_ref[h:h + 1, 0:1]) for h in range(H)]
        c_state = [jnp.where(fresh, 0.0, state_ref[h]) for h in range(H)]

    s = [jnp.dot(q[h], kt_ref[h * D:(h + 1) * D, rows], preferred_element_type=F32)
         for h in range(H)]
    inter = [jnp.dot(q[h], c_state[h].astype(BF16), preferred_element_type=F32)
             for h in range(H)]
    yield

    b_last = [b_r[h][:, L - 1:L] for h in range(H)]
    g_r = [b_last[h] - b_r[h] + li_r[h] for h in range(H)]
    m_new = [jnp.maximum(b_last[h] + m_prev[h], jnp.max(g_r[h], axis=-1, keepdims=True))
             for h in range(H)]
    kt_scaled = [(kt_ref[h * D:(h + 1) * D, rows].astype(F32)
                  * jnp.exp(g_r[h] - m_new[h])).astype(BF16) for h in range(H)]
    for h in range(H):
        decay = jnp.exp(b_last[h] + m_prev[h] - m_new[h])
        state_ref[h] = decay * c_state[h] + jnp.dot(kt_scaled[h], v_aug[h],
                                                    preferred_element_type=F32)
        m_ref[h:h + 1, :] = jnp.broadcast_to(m_new[h], (1, LANES))
    yield

    e = [jnp.where(causal, li_r[h] - b_r[h], NEG_BIG) for h in range(H)]
    u = [jnp.maximum(m_prev[h], jnp.max(e[h], axis=-1, keepdims=True)) for h in range(H)]
    u_rep = [jnp.broadcast_to(u[h], (L, D)) for h in range(H)]
    smat = [(s[h] * jnp.exp(e[h] - jnp.concatenate([u_rep[h]] * (L // D), axis=1))).astype(BF16)
            for h in range(H)]
    intra = [jnp.dot(smat[h], v_aug[h], preferred_element_type=F32) for h in range(H)]
    yield

    w_rep = [jnp.exp(m_prev[h] - u_rep[h]) for h in range(H)]
    num = [w_rep[h] * inter[h][:, :D] + intra[h][:, :D] for h in range(H)]
    den = [w_rep[h] * inter[h][:, D:] + intra[h][:, D:] for h in range(H)]
    floor_rep = [jnp.broadcast_to(jnp.exp(-(b_c[h] + u[h])), (L, D)) for h in range(H)]
    msq = [jnp.dot((num[h] * num[h]).astype(BF16), mean_mat, preferred_element_type=F32)
           for h in range(H)]
    for h in range(H):
        hs = slice(h * D, (h + 1) * D)
        rcp = 1.0 / jnp.maximum(jnp.abs(den[h]), floor_rep[h])
        f = rcp * lax.rsqrt(rcp * rcp * msq[h] + EPS)
        y_ref[rows, hs] = (jax.nn.sigmoid(o_ref[rows, hs].astype(F32)) * (num[h] * f)
                           * ng_ref[:, hs]).astype(BF16)
    yield


def _attn_kernel(qt_ref, k_ref, vt_ref, lam_ref, sg_ref, y_ref, m_ref, acc_ref, s0_ref, s1_ref,
                 qz_ref):
    TQ = ATTN_Q_BLOCK
    TK = ATTN_K_BLOCK
    D = A_HEAD_DIM
    i = pl.program_id(1)
    sub = lax.broadcasted_iota(jnp.int32, (D, 1), 0)
    for h in range(A_HEADS):
        qt = qt_ref[h * D:(h + 1) * D, :]
        zero = jnp.zeros_like(qt)
        qz_ref[h, :, 0:TQ] = jnp.where(sub < A_QK_DIM, qt, zero)
        qz_ref[h, :, TQ:2 * TQ] = jnp.where(sub >= A_QK_DIM, qt, zero)
    m_ref[...] = jnp.full_like(m_ref, NEG_BIG)
    acc_ref[...] = jnp.zeros_like(acc_ref)

    hi_cols = (slice(TK, TQ), slice(TQ + TK, 2 * TQ))

    def score_head(h, t, s_ref):
        start = pl.multiple_of(t * TK, TK)
        k = k_ref[pl.ds(start, TK), h * D:(h + 1) * D]
        s_ref[h] = jnp.dot(k, qz_ref[h], preferred_element_type=F32).astype(BF16)

    def update_head(h, t, s, cols=None):
        start = pl.multiple_of(t * TK, TK)
        m_old = m_ref[h:h + 1, :] if cols is None else \
            jnp.concatenate([m_ref[h:h + 1, c] for c in cols], axis=1)
        m_new = jnp.maximum(m_old, jnp.max(s, axis=0, keepdims=True).astype(F32))
        alpha = jnp.exp2(m_old - m_new)
        prob = jnp.exp2(s - m_new.astype(BF16))
        vt = vt_ref[h * VT_ROWS:(h + 1) * VT_ROWS, pl.ds(start, TK)]
        pv = jnp.dot(vt, prob, preferred_element_type=F32)
        if cols is None:
            m_ref[h:h + 1, :] = m_new
            acc_ref[h] = pv + alpha * acc_ref[h]
        else:
            for n, c in enumerate(cols):
                blk = slice(n * TK, (n + 1) * TK)
                m_ref[h:h + 1, c] = m_new[:, blk]
                acc_ref[h, :, c] = alpha[:, blk] * acc_ref[h, :, c] + pv[:, blk]

    for h in range(A_HEADS):
        score_head(h, 0, s0_ref)

    def body(jj, carry):
        for h in range(A_HEADS):
            score_head(h, 2 * jj + 1, s1_ref)
            update_head(h, 2 * jj, s0_ref[h])
        for h in range(A_HEADS):
            score_head(h, 2 * jj + 2, s0_ref)
            update_head(h, 2 * jj + 1, s1_ref[h])
        return carry

    lax.fori_loop(0, i, body, 0)

    visible = lax.broadcasted_iota(jnp.int32, (TK, TK), 0) <= \
        lax.broadcasted_iota(jnp.int32, (TK, TK), 1)
    start_b = pl.multiple_of((2 * i + 1) * TK, TK)
    for h in range(A_HEADS):
        k = k_ref[pl.ds(start_b, TK), h * D:(h + 1) * D]
        for n, c in enumerate(hi_cols):
            s1_ref[h, :, n * TK:(n + 1) * TK] = jnp.dot(
                k, qz_ref[h, :, c], preferred_element_type=F32).astype(BF16)

    def diag_first(h):
        s = s0_ref[h]
        parts = [s[:, n * TK:(n + 1) * TK] for n in range(2 * TQ // TK)]
        for n in (0, TQ // TK):
            parts[n] = jnp.where(visible, parts[n], NEG_BIG)
        return jnp.concatenate(parts, axis=1)

    def diag_second(h):
        return jnp.concatenate([jnp.where(visible, s1_ref[h, :, n * TK:(n + 1) * TK], NEG_BIG)
                                for n in range(len(hi_cols))], axis=1)

    for h in range(A_HEADS):
        update_head(h, 2 * i, diag_first(h))
    for h in range(A_HEADS):
        update_head(h, 2 * i + 1, diag_second(h), hi_cols)

    lq1, lk1, lq2, lk2 = lam_ref[0:1, :], lam_ref[1:2, :], lam_ref[2:3, :], lam_ref[3:4, :]
    lam = jnp.exp(jnp.sum(lq1 * lk1, axis=-1, keepdims=True)) \
        - jnp.exp(jnp.sum(lq2 * lk2, axis=-1, keepdims=True)) + LAMBDA_INIT
    for h in range(A_HEADS):
        hs = slice(h * D, (h + 1) * D)
        acc = acc_ref[h]
        ot = acc[:D, :] / acc[D:D + 1, :]
        o = (ot[:, :TQ] - lam * ot[:, TQ:]).T
        on = o * lax.rsqrt(jnp.mean(o * o, axis=-1, keepdims=True) + EPS) * sg_ref[:, hs]
        y_ref[:, hs] = (on * (1.0 - LAMBDA_INIT)).astype(BF16)


def _attention(z, q_t, v_t, lam_vecs, sub_g, batch, seq):
    TQ = ATTN_Q_BLOCK
    nq = seq // TQ
    tokens = batch * seq
    score_buf = pltpu.VMEM((A_HEADS, ATTN_K_BLOCK, 2 * TQ), BF16)
    return pl.pallas_call(
        _attn_kernel,
        grid=(batch, nq),
        in_specs=[pl.BlockSpec((A_WIDTH, TQ), lambda b, i: (0, b * nq + i)),
                  pl.BlockSpec((seq, A_WIDTH), lambda b, i: (b, COL_AK)),
                  pl.BlockSpec((A_HEADS * VT_ROWS, seq), lambda b, i: (0, b)),
                  _const_spec((4, LANES)), _const_spec((1, A_WIDTH))],
        out_specs=pl.BlockSpec((TQ, A_WIDTH), lambda b, i: (b * nq + i, 0)),
        out_shape=jax.ShapeDtypeStruct((tokens, A_WIDTH), BF16),
        scratch_shapes=[pltpu.VMEM((2 * A_HEADS, 2 * TQ), F32),
                        pltpu.VMEM((A_HEADS, VT_ROWS, 2 * TQ), F32),
                        score_buf, score_buf,
                        pltpu.VMEM((A_HEADS, A_HEAD_DIM, 2 * TQ), BF16)],
        compiler_params=pltpu.CompilerParams(dimension_semantics=("parallel", "arbitrary"),
                                             vmem_limit_bytes=VMEM_LIMIT),
        name="diff_attn",
    )(q_t, z, v_t, lam_vecs, sub_g)


def _rms(x, g):
    return x * lax.rsqrt(jnp.mean(x * x, axis=-1, keepdims=True) + EPS) * g


def _ffn_kernel(x_ref, ya_ref, p_ref, wo_ref, g2_ref, wu_ref, wd_ref, g3_ref, wg_ref, wp_ref,
                q0_ref, kt0_ref, v0_ref, o0_ref, gc0_ref, gr0_ref,
                q1_ref, kt1_ref, v1_ref, o1_ref, gc1_ref, gr1_ref, ng_ref, cw_ref, cb_ref,
                out_ref, ymc_ref, ymn_ref, state_ref, m_ref, *, blocks_per_seq):
    j = pl.program_id(0)
    chunks = x_ref.shape[0] // MLSTM_CHUNK
    cur = (q0_ref, kt0_ref, v0_ref, o0_ref, gc0_ref, gr0_ref, ng_ref, cw_ref, cb_ref)
    nxt = (q1_ref, kt1_ref, v1_ref, o1_ref, gc1_ref, gr1_ref, ng_ref, cw_ref, cb_ref)
    zero_hist = jnp.zeros((CARRY_ROWS, M_WIDTH), BF16)

    @pl.when(j == 0)
    def _():
        state_ref[...] = jnp.zeros_like(state_ref)
        m_ref[...] = jnp.zeros_like(m_ref)
        for r in range(chunks):
            for _ in _mlstm_chunk(r, *cur, ymn_ref, state_ref, m_ref, None,
                                  zero_hist if r == 0 else None):
                pass

    ymc_ref[...] = ymn_ref[...]
    next_fresh = (j + 1) % blocks_per_seq == 0

    def mlstm_phases():
        for r in range(chunks):
            hist = jnp.where(next_fresh, zero_hist, q0_ref[x_ref.shape[0] - CARRY_ROWS:, :])
            yield from _mlstm_chunk(r, *nxt, ymn_ref, state_ref, m_ref,
                                    next_fresh if r == 0 else None, hist if r == 0 else None)

    phases = mlstm_phases()

    def advance(n):
        for _ in range(n):
            next(phases, None)

    h = x_ref[...] \
        + jnp.dot(ymc_ref[...], wo_ref[0:M_WIDTH, :], preferred_element_type=F32) \
        + jnp.dot(ya_ref[...], wo_ref[M_WIDTH:, :], preferred_element_type=F32)
    advance(1)
    u2 = _rms(h, g2_ref[...]).astype(BF16)
    mlp = None
    for c in range(wu_ref.shape[1] // FF_CHUNK):
        cols = slice(c * FF_CHUNK, (c + 1) * FF_CHUNK)
        hid = jnp.maximum(jnp.dot(u2, wu_ref[:, cols], preferred_element_type=F32), 0.0)
        advance(2)
        part = jnp.dot((hid * hid).astype(BF16), wd_ref[cols, :], preferred_element_type=F32)
        advance(1)
        mlp = part if mlp is None else mlp + part
    h = h + mlp
    u3 = _rms(h, g3_ref[...]).astype(BF16)
    gate = jax.nn.sigmoid(jnp.dot(u3, wg_ref[...], preferred_element_type=F32))
    advance(2)
    e = jnp.dot(p_ref[...].astype(BF16), wp_ref[...], preferred_element_type=F32)
    advance(4 * chunks)
    out_ref[...] = h + gate * e


def _ffn(x2, y_a, p2, w_out, g2, w_up, w_down, g3, w_gate, w_ple, z, k_t, gates_c, gates_r,
         norm_g, conv_w, conv_b, seq):
    tokens, d_model = x2.shape
    tm = FFN_ROWS
    n = tokens // tm
    blk = M_WIDTH
    row_spec = lambda w: pl.BlockSpec((tm, w), lambda i: (i, 0))
    nxt = lambda i: jnp.minimum(i + 1, n - 1)
    mlstm_specs = lambda at: [
        pl.BlockSpec((tm, blk), lambda i: (at(i), COL_MQ)),
        pl.BlockSpec((blk, tm), lambda i: (0, at(i))),
        pl.BlockSpec((tm, blk), lambda i: (at(i), COL_MV)),
        pl.BlockSpec((tm, blk), lambda i: (at(i), COL_MO)),
        pl.BlockSpec((tm, LANES), lambda i: (at(i), 0)),
        pl.BlockSpec((2 * M_HEADS, tm), lambda i: (0, at(i)))]
    mlstm_args = [z, k_t, z, z, gates_c, gates_r]
    return pl.pallas_call(
        functools.partial(_ffn_kernel, blocks_per_seq=seq // tm),
        grid=(n,),
        in_specs=[row_spec(d_model), row_spec(A_WIDTH), row_spec(p2.shape[1]),
                  _const_spec(w_out.shape), _const_spec((1, d_model)), _const_spec(w_up.shape),
                  _const_spec(w_down.shape), _const_spec((1, d_model)),
                  _const_spec(w_gate.shape), _const_spec(w_ple.shape)]
        + mlstm_specs(lambda i: i) + mlstm_specs(nxt)
        + [_const_spec((1, blk)), _const_spec((CONV_WIDTH, blk)), _const_spec((1, blk))],
        out_specs=row_spec(d_model),
        out_shape=jax.ShapeDtypeStruct((tokens, d_model), F32),
        scratch_shapes=[pltpu.VMEM((tm, blk), BF16), pltpu.VMEM((tm, blk), BF16),
                        pltpu.VMEM((M_HEADS, M_HEAD_DIM, 2 * M_HEAD_DIM), F32),
                        pltpu.VMEM((2 * M_HEADS, LANES), F32)],
        compiler_params=pltpu.CompilerParams(dimension_semantics=("arbitrary",),
                                             vmem_limit_bytes=BIG_VMEM_LIMIT),
        name="ffn",
    )(x2, y_a, p2, w_out, g2, w_up, w_down, g3, w_gate, w_ple, *mlstm_args, *mlstm_args, norm_g,
      conv_w[:, :blk], conv_b[:, :blk])


def _rope_tables(seq):
    pos = np.arange(seq, dtype=np.float32)
    inv_freq = (ROPE_THETA ** (-np.arange(0, ROPE_DIM, 2, dtype=np.float32) / ROPE_DIM)
                ).astype(np.float32)
    ang = pos[:, None] * inv_freq[None, :]
    cos, sin = np.cos(ang).astype(np.float32), np.sin(ang).astype(np.float32)
    pad = np.zeros((seq, A_QK_DIM - ROPE_DIM), np.float32)
    zero8 = np.zeros((seq, ROPE_HALF), np.float32)
    ct = np.concatenate([cos, cos, pad + 1.0], axis=1)
    su = np.concatenate([zero8, sin, pad], axis=1)
    sd = np.concatenate([-sin, zero8, pad], axis=1)
    rep = LANES // A_QK_DIM
    return tuple(jnp.asarray(np.tile(t, (1, rep))) for t in (ct, su, sd))


def kernel(x, p, attn_norm_g, w_in, conv_w, conv_b, igate_b, fgate_b, mlstm_norm_g, q_norm_g,
           k_norm_g, lambda_q1, lambda_k1, lambda_q2, lambda_k2, attn_sub_norm_g, w_out,
           mlp_norm_g, w_up, w_down, ple_norm_g, w_ple_gate, w_ple_proj):
    batch, seq, d_model = x.shape
    assert w_in.shape[0] == 1, "single-layer block"
    assert seq % PROJ_ROWS == 0 and seq % FFN_ROWS == 0 and seq % ATTN_Q_BLOCK == 0
    assert FFN_ROWS % MLSTM_CHUNK == 0
    tokens = batch * seq
    x2 = x.reshape(tokens, d_model).astype(F32)
    p2 = p.reshape(tokens, p.shape[-1])

    w_bf = w_in[0].astype(BF16)
    gate_b = jnp.pad(jnp.concatenate([igate_b[0], fgate_b[0]]),
                     (0, LANES - 2 * M_HEADS)).reshape(1, LANES).astype(F32)
    rep = LANES // A_QK_DIM
    q_gain = jnp.tile(q_norm_g[0].astype(F32), rep).reshape(1, LANES)
    k_gain = jnp.tile(k_norm_g[0].astype(F32), rep).reshape(1, LANES)
    ct, su, sd = _rope_tables(seq)

    z, gates_c, gates_r, q_t, v_t, k_t = _project(
        x2, attn_norm_g[0].reshape(1, d_model).astype(F32), w_bf, gate_b, q_gain,
        k_gain, ct, su, sd, conv_w[0].astype(F32), conv_b[0].reshape(1, -1).astype(F32), seq)

    lam_vecs = jnp.pad(jnp.stack([lambda_q1[0], lambda_k1[0], lambda_q2[0], lambda_k2[0]]),
                       ((0, 0), (0, LANES - A_QK_DIM))).astype(F32)
    y_a = _attention(z, q_t, v_t, lam_vecs, attn_sub_norm_g[0].reshape(1, -1).astype(F32), batch, seq)

    out = _ffn(x2, y_a, p2, w_out[0].astype(BF16),
               mlp_norm_g[0].reshape(1, d_model).astype(F32), w_up[0].astype(BF16),
               w_down[0].astype(BF16), ple_norm_g[0].reshape(1, d_model).astype(F32),
               w_ple_gate[0].astype(BF16), w_ple_proj[0].astype(BF16),
               z, k_t, gates_c, gates_r, mlstm_norm_g[0].reshape(1, -1).astype(F32),
               conv_w[0].astype(F32), conv_b[0].reshape(1, -1).astype(F32), seq)
    return out.reshape(batch, seq, d_model).astype(x.dtype)
```
